```python
import math
import functools
import jax
import jax.numpy as jnp
from jax import lax
import numpy as np

D_MODEL = 1024
BATCH = 4
SEQ = 4096
DEPTH = 2
DEC_BATCH = 128
DEC_SEQ = 4
PAST_LEN = 8192
PAGE_SIZE = 128

HEAD_DIM = 64
SSM_HEAD_DIM = 64
SSM_HEADS = D_MODEL // SSM_HEAD_DIM
SSM_WIDTH = SSM_HEADS * SSM_HEAD_DIM
SSM_STATE = 128
SSM_GROUPS = 2
SSM_CHUNK = 128
CONV_K = 4
CONV_DIM = SSM_WIDTH + 2 * SSM_GROUPS * SSM_STATE
SWA_HEADS = D_MODEL // HEAD_DIM
SWA_KV_HEADS = 4
WINDOW = 128
FOX_HEADS = D_MODEL // HEAD_DIM
FOX_KV_HEADS = 4
FOX_Q_BLOCK = 128
FOX_FORGET_BIAS = 3.0
MEM_LEN = 256
XA_HEADS = 4
XA_HEAD_DIM = 128
XA_WIDTH = XA_HEADS * XA_HEAD_DIM
FFN_HIDDEN = ((8 * D_MODEL + 3 * 256 - 1) // (3 * 256)) * 256
N_EVEN = (DEPTH + 1) // 2
N_ODD = DEPTH // 2
EVEN_PROJ = SSM_WIDTH + CONV_DIM + SSM_HEADS + (SWA_HEADS + 2 * SWA_KV_HEADS) * HEAD_DIM
ODD_PROJ = (FOX_HEADS + 2 * FOX_KV_HEADS) * HEAD_DIM + FOX_HEADS
RMS_EPS = 1e-6
NEG_INF = -1e30

kernel_name = 'hybrid_ssd_swa_fox_step'


def rms_norm(x, g):
    xf = x.astype(jnp.float32)
    y = xf * lax.rsqrt(jnp.mean(xf * xf, axis=-1, keepdims=True) + RMS_EPS)
    return (y * g.astype(jnp.float32)).astype(x.dtype)


def gqa_attend(q, k, v, mask=None, bias=None, sink=None):
    s = jnp.einsum('...qkgd,...skd->...kgqs', q, k, preferred_element_type=jnp.float32)
    s = s * (q.shape[-1] ** -0.5)
    if bias is not None:
        s = s + bias
    if mask is not None:
        s = jnp.where(mask, s, NEG_INF)
    if sink is not None:
        sk = jnp.broadcast_to(sink.astype(jnp.float32)[:, :, None, None], s.shape[:-1] + (1,))
        p = jax.nn.softmax(jnp.concatenate([s, sk], axis=-1), axis=-1)[..., :-1]
    else:
        p = jax.nn.softmax(s, axis=-1)
    return jnp.einsum('...kgqs,...skd->...qkgd', p.astype(v.dtype), v)


def causal_conv(xpad, w, b):
    t = xpad.shape[1] - (CONV_K - 1)
    y = b
    for j in range(CONV_K):
        y = y + xpad[:, j:j + t] * w[j]
    return y


def ssd_chunk(h, inp, a_neg):
    x, dt, bm, cm = inp
    L = x.shape[1]
    rep = SSM_HEADS // SSM_GROUPS
    bh = jnp.repeat(bm, rep, axis=2)
    ch = jnp.repeat(cm, rep, axis=2)
    cs = jnp.cumsum(dt * a_neg, axis=1)
    causal = jnp.tril(jnp.ones((L, L), dtype=bool))
    decay = jnp.exp(jnp.where(causal[None, :, :, None], cs[:, :, None, :] - cs[:, None, :, :], NEG_INF))
    xdt = x * dt[..., None]
    scores = jnp.einsum('bthn,bshn->btsh', ch, bh) * decay
    y = jnp.einsum('btsh,bshp->bthp', scores, xdt)
    y = y + jnp.einsum('bthn,bhpn->bthp', ch, h) * jnp.exp(cs)[..., None]
    w_end = jnp.exp(cs[:, -1:, :] - cs)
    h_new = h * jnp.exp(cs[:, -1, :])[:, :, None, None] + jnp.einsum('bsh,bshn,bshp->bhpn', w_end, bh, xdt)
    return h_new, y


def ssm_branch(z, xbc_pad, dt_raw, h0, conv_w, conv_b, dt_bias, a_log, d_skip, ssm_norm):
    xbc = jax.nn.silu(causal_conv(xbc_pad, conv_w, conv_b)).astype(jnp.float32)
    b, t, _ = xbc.shape
    gn = SSM_GROUPS * SSM_STATE
    xs = xbc[..., :SSM_WIDTH].reshape(b, t, SSM_HEADS, SSM_HEAD_DIM)
    bm = xbc[..., SSM_WIDTH:SSM_WIDTH + gn].reshape(b, t, SSM_GROUPS, SSM_STATE)
    cm = xbc[..., SSM_WIDTH + gn:].reshape(b, t, SSM_GROUPS, SSM_STATE)
    dt = jax.nn.softplus(dt_raw.astype(jnp.float32) + dt_bias.astype(jnp.float32))
    a_neg = -jnp.exp(a_log.astype(jnp.float32))
    L = min(SSM_CHUNK, t)
    nc = t // L

    def chunks(u):
        return jnp.swapaxes(u.reshape((b, nc, L) + u.shape[2:]), 0, 1)

    h, ys = lax.scan(functools.partial(ssd_chunk, a_neg=a_neg), h0.astype(jnp.float32),
                     (chunks(xs), chunks(dt), chunks(bm), chunks(cm)))
    y = jnp.swapaxes(ys, 0, 1).reshape(b, t, SSM_HEADS, SSM_HEAD_DIM)
    y = (y + d_skip.astype(jnp.float32)[:, None] * xs).reshape(b, t, SSM_WIDTH)
    y = rms_norm(y * jax.nn.silu(z.astype(jnp.float32)), ssm_norm)
    return y.astype(z.dtype), h


def even_split(proj):
    b, t, _ = proj.shape
    sizes = [SSM_WIDTH, CONV_DIM, SSM_HEADS, SWA_HEADS * HEAD_DIM, SWA_KV_HEADS * HEAD_DIM]
    z, xbc, dt, q, k, v = jnp.split(proj, [int(i) for i in np.cumsum(sizes)], axis=-1)
    q = q.reshape(b, t, SWA_KV_HEADS, SWA_HEADS // SWA_KV_HEADS, HEAD_DIM)
    k = k.reshape(b, t, SWA_KV_HEADS, HEAD_DIM)
    v = v.reshape(b, t, SWA_KV_HEADS, HEAD_DIM)
    return z, xbc, dt, q, k, v


def swa_prompt(q, k, v, sink):
    b, t = q.shape[:2]
    nb = t // WINDOW
    qb = q.reshape((b, nb, WINDOW) + q.shape[2:])
    kb = k.reshape((b, nb, WINDOW) + k.shape[2:])
    vb = v.reshape((b, nb, WINDOW) + v.shape[2:])

    def with_prev(u):
        prev = jnp.concatenate([jnp.zeros_like(u[:, :1]), u[:, :-1]], axis=1)
        return jnp.concatenate([prev, u], axis=2)

    blk = jnp.arange(nb)[:, None] * WINDOW
    qpos = blk + jnp.arange(WINDOW)[None]
    kpos = blk - WINDOW + jnp.arange(2 * WINDOW)[None]
    diff = qpos[:, :, None] - kpos[:, None, :]
    mask = (diff >= 0) & (diff < WINDOW) & (kpos[:, None, :] >= 0)
    o = gqa_attend(qb, with_prev(kb), with_prev(vb), mask=mask[:, None, None], sink=sink)
    return o.reshape(b, t, SWA_HEADS * HEAD_DIM)


def swa_sample(q, k, v, ck, cv, sink):
    b, s = q.shape[:2]
    kk = jnp.concatenate([ck.astype(k.dtype), k], axis=1)
    vv = jnp.concatenate([cv.astype(v.dtype), v], axis=1)
    diff = jnp.arange(s)[:, None] - (jnp.arange(WINDOW + s) - WINDOW)[None]
    mask = (diff >= 0) & (diff < WINDOW)
    o = gqa_attend(q, kk, vv, mask=mask, sink=sink)
    return o.reshape(b, s, SWA_HEADS * HEAD_DIM), kk[:, -WINDOW:], vv[:, -WINDOW:]


def odd_split(proj, fb):
    b, t, _ = proj.shape
    sizes = [FOX_HEADS * HEAD_DIM, FOX_KV_HEADS * HEAD_DIM, FOX_KV_HEADS * HEAD_DIM]
    q, k, v, fg = jnp.split(proj, [int(i) for i in np.cumsum(sizes)], axis=-1)
    q = q.reshape(b, t, FOX_KV_HEADS, FOX_HEADS // FOX_KV_HEADS, HEAD_DIM)
    k = k.reshape(b, t, FOX_KV_HEADS, HEAD_DIM)
    v = v.reshape(b, t, FOX_KV_HEADS, HEAD_DIM)
    logf = jax.nn.log_sigmoid(fg.astype(jnp.float32) + fb.astype(jnp.float32))
    return q, k, v, logf


def fox_attend(q, k, v, cum, q_off):
    b, tq = q.shape[:2]
    tk = k.shape[1]
    c = jnp.transpose(cum.reshape(b, tk, FOX_KV_HEADS, FOX_HEADS // FOX_KV_HEADS), (0, 2, 3, 1))
    cq = lax.dynamic_slice_in_dim(c, q_off, tq, axis=3)
    bias = cq[..., :, None] - c[..., None, :]
    mask = jnp.arange(tk)[None, :] <= (q_off + jnp.arange(tq))[:, None]
    return gqa_attend(q, k, v, mask=mask, bias=bias)


def fox_prompt(q, k, v, logf):
    b, t = q.shape[:2]
    cum = jnp.cumsum(logf, axis=1)
    nb = t // FOX_Q_BLOCK
    qb = jnp.swapaxes(q.reshape((b, nb, FOX_Q_BLOCK) + q.shape[2:]), 0, 1)
    offs = jnp.arange(nb) * FOX_Q_BLOCK
    o = lax.map(lambda a: fox_attend(a[0], k, v, cum, a[1]), (qb, offs))
    return jnp.swapaxes(o, 0, 1).reshape(b, t, FOX_HEADS * HEAD_DIM)


def mem_kv(mem, g, wk, wv):
    b, m, _ = mem.shape
    mn = rms_norm(mem, g)
    return ((mn @ wk).reshape(b, m, XA_HEADS, XA_HEAD_DIM),
            (mn @ wv).reshape(b, m, XA_HEADS, XA_HEAD_DIM))


def cross_attn(xn, mk, mv, wq, wo):
    b, t, _ = xn.shape
    q = (xn @ wq).reshape(b, t, XA_HEADS, 1, XA_HEAD_DIM)
    o = gqa_attend(q, mk.astype(q.dtype), mv.astype(q.dtype))
    return o.reshape(b, t, XA_WIDTH) @ wo


def swiglu(xn, w_in, w_out):
    g, u = jnp.split(xn @ w_in, 2, axis=-1)
    return (jax.nn.silu(g) * u) @ w_out


def setup_inputs(seed: int = 0) -> dict:
    key = jax.random.key(seed)
    counter = [0]

    def nk():
        counter[0] += 1
        return jax.random.fold_in(key, counter[0])

    def nrm(shape, scale=1.0):
        return jax.random.normal(nk(), shape, jnp.float32) * scale

    def gain(shape):
        return 1.0 + nrm(shape, 0.02)

    n_pages = PAST_LEN // PAGE_SIZE
    n_pool = (DEC_BATCH * n_pages * 5) // 4
    page_table = jax.random.permutation(nk(), n_pool)[:DEC_BATCH * n_pages].reshape(DEC_BATCH, n_pages).astype(jnp.int32)
    dt0 = jnp.exp(jax.random.uniform(nk(), (N_EVEN, SSM_HEADS), jnp.float32, math.log(1e-3), math.log(1e-1)))
    a_log = jnp.log(jax.random.uniform(nk(), (N_EVEN, SSM_HEADS), jnp.float32, 1.0, 16.0))
    out_even_in = SSM_WIDTH + SWA_HEADS * HEAD_DIM
    return {
        'x_prompt': nrm((BATCH, SEQ, D_MODEL)),
        'x_sample': nrm((DEC_BATCH, DEC_SEQ, D_MODEL)),
        'state_ssm': nrm((N_EVEN, DEC_BATCH, SSM_HEADS, SSM_HEAD_DIM, SSM_STATE), 0.1),
        'state_conv': nrm((N_EVEN, DEC_BATCH, CONV_K - 1, CONV_DIM)),
        'cache_swa_k': nrm((N_EVEN, DEC_BATCH, WINDOW, SWA_KV_HEADS, HEAD_DIM)),
        'cache_swa_v': nrm((N_EVEN, DEC_BATCH, WINDOW, SWA_KV_HEADS, HEAD_DIM)),
        'cache_fox_k': nrm((N_ODD, n_pool, PAGE_SIZE, FOX_KV_HEADS, HEAD_DIM)),
        'cache_fox_v': nrm((N_ODD, n_pool, PAGE_SIZE, FOX_KV_HEADS, HEAD_DIM)),
        'cache_fox_logf': jax.nn.log_sigmoid(FOX_FORGET_BIAS + nrm((N_ODD, n_pool, PAGE_SIZE, FOX_HEADS))),
        'cache_mem_k': nrm((DEPTH, DEC_BATCH, MEM_LEN, XA_HEADS, XA_HEAD_DIM)),
        'cache_mem_v': nrm((DEPTH, DEC_BATCH, MEM_LEN, XA_HEADS, XA_HEAD_DIM)),
        'page_table': page_table,
        'mem_prompt': nrm((BATCH, MEM_LEN, D_MODEL)),
        'norm_mix': gain((DEPTH, D_MODEL)),
        'norm_xa': gain((DEPTH, D_MODEL)),
        'norm_mem': gain((DEPTH, D_MODEL)),
        'norm_ffn': gain((DEPTH, D_MODEL)),
        'w_in_even': nrm((N_EVEN, D_MODEL, EVEN_PROJ), D_MODEL ** -0.5),
        'conv_w': nrm((N_EVEN, CONV_K, CONV_DIM), CONV_K ** -0.5),
        'conv_b': nrm((N_EVEN, CONV_DIM), 0.02),
        'dt_bias': dt0 + jnp.log(-jnp.expm1(-dt0)),
        'a_log': a_log,
        'd_skip': 1.0 + nrm((N_EVEN, SSM_HEADS), 0.1),
        'ssm_norm': gain((N_EVEN, SSM_WIDTH)),
        'swa_sink': nrm((N_EVEN, SWA_HEADS), 0.5),
        'w_out_even': nrm((N_EVEN, out_even_in, D_MODEL), out_even_in ** -0.5),
        'w_in_odd': nrm((N_ODD, D_MODEL, ODD_PROJ), D_MODEL ** -0.5),
        'fox_fb': FOX_FORGET_BIAS + nrm((N_ODD, FOX_HEADS), 0.1),
        'w_out_odd': nrm((N_ODD, FOX_HEADS * HEAD_DIM, D_MODEL), (FOX_HEADS * HEAD_DIM) ** -0.5),
        'w_xq': nrm((DEPTH, D_MODEL, XA_WIDTH), D_MODEL ** -0.5),
        'w_xk': nrm((DEPTH, D_MODEL, XA_WIDTH), D_MODEL ** -0.5),
        'w_xv': nrm((DEPTH, D_MODEL, XA_WIDTH), D_MODEL ** -0.5),
        'w_xo': nrm((DEPTH, XA_WIDTH, D_MODEL), XA_WIDTH ** -0.5),
        'w_ffn_in': nrm((DEPTH, D_MODEL, 2 * FFN_HIDDEN), D_MODEL ** -0.5),
        'w_ffn_out': nrm((DEPTH, FFN_HIDDEN, D_MODEL), FFN_HIDDEN ** -0.5),
        'norm_final': gain((D_MODEL,)),
    }


def reference(x_prompt, x_sample, state_ssm, state_conv, cache_swa_k, cache_swa_v,
              cache_fox_k, cache_fox_v, cache_fox_logf, cache_mem_k, cache_mem_v,
              page_table, mem_prompt, norm_mix, norm_xa, norm_mem, norm_ffn,
              w_in_even, conv_w, conv_b, dt_bias, a_log, d_skip, ssm_norm, swa_sink,
              w_out_even, w_in_odd, fox_fb, w_out_odd, w_xq, w_xk, w_xv, w_xo,
              w_ffn_in, w_ffn_out, norm_final):
    xp, xs = x_prompt, x_sample
    bp, bs = xp.shape[0], xs.shape[0]
    past = page_table.shape[1] * PAGE_SIZE
    p_ssm_l, p_conv_l, p_swk_l, p_swv_l = [], [], [], []
    s_ssm_l, s_conv_l, s_swk_l, s_swv_l = [], [], [], []
    p_fk_l, p_fv_l, p_fl_l, s_fk_l, s_fv_l, s_fl_l = [], [], [], [], [], []
    p_mk_l, p_mv_l = [], []
    for l in range(DEPTH):
        li = l // 2
        hp = rms_norm(xp, norm_mix[l])
        hs = rms_norm(xs, norm_mix[l])
        if l % 2 == 0:
            ssm_params = (conv_w[li], conv_b[li], dt_bias[li], a_log[li], d_skip[li], ssm_norm[li])
            sink = swa_sink[li].reshape(SWA_KV_HEADS, SWA_HEADS // SWA_KV_HEADS)
            z, xbc, dt, q, k, v = even_split(hp @ w_in_even[li])
            xbc_pad = jnp.concatenate([jnp.zeros((bp, CONV_K - 1, CONV_DIM), xbc.dtype), xbc], axis=1)
            h0 = jnp.zeros((bp, SSM_HEADS, SSM_HEAD_DIM, SSM_STATE), jnp.float32)
            y_ssm, h = ssm_branch(z, xbc_pad, dt, h0, *ssm_params)
            y_att = swa_prompt(q, k, v, sink)
            xp = xp + jnp.concatenate([y_ssm, y_att.astype(y_ssm.dtype)], axis=-1) @ w_out_even[li]
            p_ssm_l.append(h)
            p_conv_l.append(xbc_pad[:, -(CONV_K - 1):])
            p_swk_l.append(k[:, -WINDOW:])
            p_swv_l.append(v[:, -WINDOW:])
            z, xbc, dt, q, k, v = even_split(hs @ w_in_even[li])
            xbc_pad = jnp.concatenate([state_conv[li].astype(xbc.dtype), xbc], axis=1)
            y_ssm, h = ssm_branch(z, xbc_pad, dt, state_ssm[li], *ssm_params)
            y_att, new_k, new_v = swa_sample(q, k, v, cache_swa_k[li], cache_swa_v[li], sink)
            xs = xs + jnp.concatenate([y_ssm, y_att.astype(y_ssm.dtype)], axis=-1) @ w_out_even[li]
            s_ssm_l.append(h)
            s_conv_l.append(xbc_pad[:, -(CONV_K - 1):])
            s_swk_l.append(new_k)
            s_swv_l.append(new_v)
        else:
            q, k, v, logf = odd_split(hp @ w_in_odd[li], fox_fb[li])
            xp = xp + fox_prompt(q, k, v, logf) @ w_out_odd[li]
            p_fk_l.append(k)
            p_fv_l.append(v)
            p_fl_l.append(logf)
            q, k, v, logf = odd_split(hs @ w_in_odd[li], fox_fb[li])
            kp = cache_fox_k[li, page_table].reshape(bs, past, FOX_KV_HEADS, HEAD_DIM)
            vp = cache_fox_v[li, page_table].reshape(bs, past, FOX_KV_HEADS, HEAD_DIM)
            lp = cache_fox_logf[li, page_table].reshape(bs, past, FOX_HEADS).astype(jnp.float32)
            kk = jnp.concatenate([kp.astype(k.dtype), k], axis=1)
            vv = jnp.concatenate([vp.astype(v.dtype), v], axis=1)
            cum = jnp.cumsum(jnp.concatenate([lp, logf], axis=1), axis=1)
            y = fox_attend(q, kk, vv, cum, past).reshape(bs, -1, FOX_HEADS * HEAD_DIM)
            xs = xs + y @ w_out_odd[li]
            s_fk_l.append(k)
            s_fv_l.append(v)
            s_fl_l.append(logf)
        mk, mv = mem_kv(mem_prompt, norm_mem[l], w_xk[l], w_xv[l])
        xp = xp + cross_attn(rms_norm(xp, norm_xa[l]), mk, mv, w_xq[l], w_xo[l])
        xs = xs + cross_attn(rms_norm(xs, norm_xa[l]), cache_mem_k[l], cache_mem_v[l], w_xq[l], w_xo[l])
        p_mk_l.append(mk)
        p_mv_l.append(mv)
        xp = xp + swiglu(rms_norm(xp, norm_ffn[l]), w_ffn_in[l], w_ffn_out[l])
        xs = xs + swiglu(rms_norm(xs, norm_ffn[l]), w_ffn_in[l], w_ffn_out[l])
    y_prompt = rms_norm(xp, norm_final)
    y_sample = rms_norm(xs, norm_final)
    return (y_prompt, y_sample,
            jnp.stack(p_ssm_l), jnp.stack(p_conv_l), jnp.stack(p_swk_l), jnp.stack(p_swv_l),
            jnp.stack(p_fk_l), jnp.stack(p_fv_l), jnp.stack(p_fl_l),
            jnp.stack(p_mk_l), jnp.stack(p_mv_l),
            jnp.stack(s_ssm_l), jnp.stack(s_conv_l), jnp.stack(s_swk_l), jnp.stack(s_swv_l),
            jnp.stack(s_fk_l), jnp.stack(s_fv_l), jnp.stack(s_fl_l))
```

```python
import functools
import math

import numpy as np
import jax
import jax.numpy as jnp
from jax import lax
from jax.experimental import pallas as pl
from jax.experimental.pallas import tpu as pltpu

F32 = jnp.float32
BF16 = jnp.bfloat16

D_MODEL = 1024
HEAD_DIM = 64
N_HEADS = 16
N_KV = 4
SSM_HEADS = 16
SSM_P = 64
SSM_WIDTH = 1024
SSM_N = 128
SSM_GROUPS = 2
SSM_CHUNK = 128
CONV_K = 4
CONV_DIM = 1536
WINDOW = 128
PAGE = 128
XA_HEADS = 4
XA_DIM = 128
XA_WIDTH = 512
FFN_H = 2816
RMS_EPS = 1e-6
NEG_INF = -1e30

V7X_VMEM_BYTES = 64 * 1024 * 1024
VMEM_LIMIT = V7X_VMEM_BYTES - 8 * 1024 * 1024
LANES = 128

NT_DIMS = (((1,), (1,)), ((), ()))


def _cparams(sem):
    return pltpu.CompilerParams(dimension_semantics=sem, vmem_limit_bytes=VMEM_LIMIT)


def _const_spec(shape):
    nd = len(shape)
    return pl.BlockSpec(shape, lambda *_: (0,) * nd, pipeline_mode=pl.Buffered(1))


def _softplus(x):
    return jnp.maximum(x, 0.0) + jnp.log1p(jnp.exp(-jnp.abs(x)))


def _silu(x):
    return x * (1.0 / (1.0 + jnp.exp(-x)))


def _split3(x):
    hi = x.astype(BF16)
    r1 = x - hi.astype(F32)
    mid = r1.astype(BF16)
    lo = (r1 - mid.astype(F32)).astype(BF16)
    return hi, mid, lo


def _dot01_left(m01, x):
    hi, mid, lo = _split3(x)
    d = lambda b: jnp.dot(m01, b, preferred_element_type=F32)
    return (d(lo) + d(mid)) + d(hi)


def _dot01_right(x, m01):
    hi, mid, lo = _split3(x)
    d = lambda a: jnp.dot(a, m01, preferred_element_type=F32)
    return (d(lo) + d(mid)) + d(hi)


def _iota(shape, dim):
    return lax.broadcasted_iota(jnp.int32, shape, dim)


def _norm_proj_kernel(*refs, segs, has_aux):
    x_ref, g_ref, w_ref = refs[:3]
    pos = 3
    aux_ref = None
    if has_aux:
        aux_ref = refs[3]
        pos = 4
    out_refs = refs[pos:]
    x = x_ref[...]
    ms = jnp.mean(x * x, axis=-1, keepdims=True)
    xn = (x * lax.rsqrt(ms + RMS_EPS)) * g_ref[...]
    xb = xn.astype(BF16)
    oi = 0
    for (c0, c1, dtypes, act) in segs:
        y = jnp.dot(xb, w_ref[:, c0:c1], preferred_element_type=F32)
        if act == "logsig":
            y = -_softplus(-(y + aux_ref[...]))
        for dt in dtypes:
            out_refs[oi][...] = y.astype(dt)
            oi += 1


def norm_proj(x, g, w, segs, aux=None, tm=512):
    m, d = x.shape
    tm = min(tm, m)
    assert m % tm == 0
    n = w.shape[1]
    in_specs = [pl.BlockSpec((tm, d), lambda i: (i, 0)), _const_spec((1, d)), _const_spec((d, n))]
    args = [x, g.reshape(1, d).astype(F32), w]
    if aux is not None:
        in_specs.append(_const_spec(aux.shape))
        args.append(aux)
    out_shape, out_specs = [], []
    for (c0, c1, dtypes, _) in segs:
        for dt in dtypes:
            out_shape.append(jax.ShapeDtypeStruct((m, c1 - c0), dt))
            out_specs.append(pl.BlockSpec((tm, c1 - c0), lambda i: (i, 0)))
    return pl.pallas_call(
        functools.partial(_norm_proj_kernel, segs=tuple(segs), has_aux=aux is not None),
        grid=(m // tm,),
        in_specs=in_specs,
        out_specs=out_specs,
        out_shape=out_shape,
        compiler_params=_cparams(("parallel",)),
        name="norm_proj",
    )(*args)


def _mm_res_kernel(*refs, n):
    a_refs, w_refs = refs[:n], refs[n:2 * n]
    res_ref, o_ref = refs[2 * n], refs[2 * n + 1]
    acc = res_ref[...]
    for a_ref, w_ref in zip(a_refs, w_refs):
        acc = acc + jnp.dot(a_ref[...].astype(BF16), w_ref[...], preferred_element_type=F32)
    o_ref[...] = acc


def matmul_residual(a_list, w_list, res, tm=512):
    m, d = res.shape
    tm = min(tm, m)
    assert m % tm == 0
    n = len(a_list)
    in_specs = [pl.BlockSpec((tm, a.shape[1]), lambda i: (i, 0)) for a in a_list]
    in_specs += [_const_spec(w.shape) for w in w_list]
    in_specs.append(pl.BlockSpec((tm, d), lambda i: (i, 0)))
    return pl.pallas_call(
        functools.partial(_mm_res_kernel, n=n),
        grid=(m // tm,),
        in_specs=in_specs,
        out_specs=pl.BlockSpec((tm, d), lambda i: (i, 0)),
        out_shape=jax.ShapeDtypeStruct((m, d), F32),
        compiler_params=_cparams(("parallel",)),
        name="matmul_residual",
    )(*a_list, *w_list, res)


FFN_CHUNK = 512


def _ffn_kernel(*refs, final_norm):
    if final_norm:
        x_ref, g_ref, wg_ref, wu_ref, wo_ref, gf_ref, o_ref, acc_ref = refs
    else:
        x_ref, g_ref, wg_ref, wu_ref, wo_ref, o_ref, acc_ref = refs
    x = x_ref[...]
    ms = jnp.mean(x * x, axis=-1, keepdims=True)
    xb = ((x * lax.rsqrt(ms + RMS_EPS)) * g_ref[...]).astype(BF16)
    hdim = wg_ref.shape[1]
    acc_ref[...] = x
    for c0 in range(0, hdim, FFN_CHUNK):
        c1 = min(c0 + FFN_CHUNK, hdim)
        gt = jnp.dot(xb, wg_ref[:, c0:c1], preferred_element_type=F32)
        up = jnp.dot(xb, wu_ref[:, c0:c1], preferred_element_type=F32)
        act = (_silu(gt) * up).astype(BF16)
        acc_ref[...] += jnp.dot(act, wo_ref[c0:c1, :], preferred_element_type=F32)
    y = acc_ref[...]
    if final_norm:
        ms2 = jnp.mean(y * y, axis=-1, keepdims=True)
        y = (y * lax.rsqrt(ms2 + RMS_EPS)) * gf_ref[...]
    o_ref[...] = y


def ffn(x, g, wg, wu, wo, g_final=None, tm=512):
    m, d = x.shape
    tm = min(tm, m)
    assert m % tm == 0
    final_norm = g_final is not None
    in_specs = [pl.BlockSpec((tm, d), lambda i: (i, 0)), _const_spec((1, d)),
                _const_spec(wg.shape), _const_spec(wu.shape), _const_spec(wo.shape)]
    args = [x, g.reshape(1, d).astype(F32), wg, wu, wo]
    if final_norm:
        in_specs.append(_const_spec((1, d)))
        args.append(g_final.reshape(1, d).astype(F32))
    return pl.pallas_call(
        functools.partial(_ffn_kernel, final_norm=final_norm),
        grid=(m // tm,),
        in_specs=in_specs,
        out_specs=pl.BlockSpec((tm, d), lambda i: (i, 0)),
        out_shape=jax.ShapeDtypeStruct((m, d), F32),
        scratch_shapes=[pltpu.VMEM((tm, d), F32)],
        compiler_params=_cparams(("parallel",)),
        name="ffn",
    )(*args)


def _ssd_kernel(*refs, L, valid, has_h0):
    if has_h0:
        (xbc_ref, z_ref, dtr_ref, tail_ref, h0_ref, cw_ref, cb_ref, dtb_ref, al_ref, ale_ref,
         dsk_ref, gn_ref, y_ref, hout_ref, xpad_scr, hT_scr) = refs
    else:
        (xbc_ref, z_ref, dtr_ref, tail_ref, cw_ref, cb_ref, dtb_ref, al_ref, ale_ref,
         dsk_ref, gn_ref, y_ref, hout_ref, xpad_scr, hT_scr) = refs
    c = pl.program_id(1)
    nc = pl.num_programs(1)

    @pl.when(c == 0)
    def _():
        xpad_scr[0:8, :] = tail_ref[0]
        if has_h0:
            hT_scr[...] = h0_ref[0].T
        else:
            hT_scr[...] = jnp.zeros(hT_scr.shape, F32)

    @pl.when(c > 0)
    def _():
        xpad_scr[0:8, :] = xpad_scr[L:L + 8, :]

    xpad_scr[8:8 + L, :] = xbc_ref[0]
    acc = cb_ref[...]
    for j in range(CONV_K):
        acc = acc + xpad_scr[5 + j:5 + j + L, :] * cw_ref[j:j + 1, :]
    xc = _silu(acc)
    xs = xc[:, :SSM_WIDTH]
    gn_w = SSM_GROUPS * SSM_N
    bm = xc[:, SSM_WIDTH:SSM_WIDTH + gn_w]
    cm = xc[:, SSM_WIDTH + gn_w:]

    dt = _softplus(dtr_ref[0] + dtb_ref[...])
    if valid < L:
        dt = jnp.where(_iota((L, LANES), 0) < valid, dt, 0.0)
    a128 = -jnp.exp(al_ref[...])
    a_exp = -jnp.exp(ale_ref[...])

    expand = (_iota((LANES, SSM_WIDTH), 1) // SSM_P == _iota((LANES, SSM_WIDTH), 0)).astype(BF16)
    tril = (_iota((L, L), 0) >= _iota((L, L), 1))
    tril_b = tril.astype(BF16)

    dt_exp = _dot01_right(dt, expand)
    cs_exp = _dot01_left(tril_b, dt_exp * a_exp)
    cs = _dot01_left(tril_b, dt * a128)
    cs_t = cs.T

    last = cs_exp[L - 1:L, :]
    ecs = jnp.exp(cs_exp)
    wend = jnp.exp(last - cs_exp)
    dec = jnp.exp(last)
    xdt = xs * dt_exp
    xdt_b = xdt.astype(BF16)
    xw_b = (xdt * wend).astype(BF16)

    lane_lo = _iota((L, LANES), 1) < SSM_P
    gw = SSM_WIDTH // SSM_GROUPS
    y_parts = []
    for g in range(SSM_GROUPS):
        bg = bm[:, g * SSM_N:(g + 1) * SSM_N]
        cg_b = cm[:, g * SSM_N:(g + 1) * SSM_N].astype(BF16)
        cb = lax.dot_general(cg_b, bg.astype(BF16), NT_DIMS, preferred_element_type=F32)
        h_old = hT_scr[:, g * gw:(g + 1) * gw]
        y_state = jnp.dot(cg_b, h_old.astype(BF16), preferred_element_type=F32)
        pairs = []
        for i in range(gw // LANES):
            slab = g * (gw // LANES) + i
            xp = xdt_b[:, slab * LANES:(slab + 1) * LANES]
            halves = []
            for half in range(2):
                h = 2 * slab + half
                diff = cs[:, h:h + 1] - cs_t[h:h + 1, :]
                dm = jnp.exp(jnp.where(tril, diff, NEG_INF))
                sc = (cb * dm).astype(BF16)
                halves.append(jnp.dot(sc, xp, preferred_element_type=F32))
            pairs.append(jnp.where(lane_lo, halves[0], halves[1]))
        y_intra = jnp.concatenate(pairs, axis=1)
        y_parts.append(y_intra + y_state * ecs[:, g * gw:(g + 1) * gw])
        bg_t = bg.T.astype(BF16)
        upd = jnp.dot(bg_t, xw_b[:, g * gw:(g + 1) * gw], preferred_element_type=F32)
        hT_scr[:, g * gw:(g + 1) * gw] = h_old * dec[:, g * gw:(g + 1) * gw] + upd
    y = jnp.concatenate(y_parts, axis=1) + dsk_ref[...] * xs
    yg = y * _silu(z_ref[0])
    ms = jnp.mean(yg * yg, axis=-1, keepdims=True)
    y_ref[0] = ((yg * lax.rsqrt(ms + RMS_EPS)) * gn_ref[...]).astype(y_ref.dtype)

    @pl.when(c == nc - 1)
    def _():
        hout_ref[0] = hT_scr[...].T


def ssd(xbc, z, dtr, tail, h0, conv_w, conv_b, dt_bias, a_log, d_skip, ssm_norm, L, valid):
    b, t, _ = xbc.shape
    assert t % L == 0
    nc = t // L
    has_h0 = h0 is not None
    pad16 = lambda v: jnp.pad(v.astype(F32), (0, LANES - v.shape[0])).reshape(1, LANES)
    rep = lambda v: jnp.repeat(v.astype(F32), SSM_P).reshape(1, SSM_WIDTH)
    row = lambda last: pl.BlockSpec((1, L, last), lambda i, j: (i, j, 0))
    per_b = lambda s1, s2: pl.BlockSpec((1, s1, s2), lambda i, j: (i, 0, 0))
    in_specs = [row(CONV_DIM), row(SSM_WIDTH), row(LANES), per_b(8, CONV_DIM)]
    args = [xbc, z, dtr, tail]
    if has_h0:
        in_specs.append(per_b(SSM_WIDTH, SSM_N))
        args.append(h0)
    consts = [conv_w.astype(F32), conv_b.reshape(1, CONV_DIM).astype(F32), pad16(dt_bias), pad16(a_log),
              rep(a_log), rep(d_skip), ssm_norm.reshape(1, SSM_WIDTH).astype(F32)]
    in_specs += [_const_spec(cst.shape) for cst in consts]
    args += consts
    return pl.pallas_call(
        functools.partial(_ssd_kernel, L=L, valid=valid, has_h0=has_h0),
        grid=(b, nc),
        in_specs=in_specs,
        out_specs=[row(SSM_WIDTH), per_b(SSM_WIDTH, SSM_N)],
        out_shape=[jax.ShapeDtypeStruct((b, t, SSM_WIDTH), BF16),
                   jax.ShapeDtypeStruct((b, SSM_WIDTH, SSM_N), F32)],
        scratch_shapes=[pltpu.VMEM((L + 8, CONV_DIM), F32), pltpu.VMEM((SSM_N, SSM_WIDTH), F32)],
        compiler_params=_cparams(("parallel", "arbitrary")),
        name="ssd",
    )(*args)


GROUP = N_HEADS // N_KV


def _head_perm():
    order = []
    for slab in range(N_HEADS // 2):
        j, g = slab // GROUP, slab % GROUP
        order += [(2 * j) * GROUP + g, (2 * j + 1) * GROUP + g]
    return np.asarray(order)


HEAD_PERM = _head_perm()
COL_PERM = (HEAD_PERM[:, None] * HEAD_DIM + np.arange(HEAD_DIM)[None, :]).reshape(-1)
SLABS_PER_STEP = 4
HEADS_PER_STEP = 2 * SLABS_PER_STEP


def _masked_q(q_ref, qm_scr):
    tq = q_ref.shape[1]
    lo = _iota((tq, LANES), 1) < HEAD_DIM
    scale = jnp.asarray(HEAD_DIM ** -0.5, BF16)
    for s in range(SLABS_PER_STEP):
        slab = q_ref[0, :, s * LANES:(s + 1) * LANES] * scale
        zero = jnp.zeros_like(slab)
        qm_scr[2 * s] = jnp.where(lo, slab, zero)
        qm_scr[2 * s + 1] = jnp.where(lo, zero, slab)


def _write_heads(o_ref, acc_scr, inv_l):
    tq = o_ref.shape[1]
    lo = _iota((tq, LANES), 1) < HEAD_DIM
    for s in range(SLABS_PER_STEP):
        a = acc_scr[2 * s] * inv_l[2 * s]
        b = acc_scr[2 * s + 1] * inv_l[2 * s + 1]
        o_ref[0, :, s * LANES:(s + 1) * LANES] = jnp.where(lo, a, b).astype(o_ref.dtype)


def _swa_prompt_kernel(q_ref, kp_ref, kc_ref, vp_ref, vc_ref, sink_ref, o_ref, qm_scr, acc_scr):
    i = pl.program_id(2)
    w = WINDOW
    _masked_q(q_ref, qm_scr)
    kcat = jnp.concatenate([kp_ref[0], kc_ref[0]], axis=0)
    vcat = jnp.concatenate([vp_ref[0], vc_ref[0]], axis=0)
    r = _iota((w, 2 * w), 0)
    col = _iota((w, 2 * w), 1)
    lim = jnp.where(i > 0, r, 2 * w)
    mask = ((col > lim) & (col < w)) | ((col >= w) & (col - w <= r))
    inv_l = []
    for h in range(HEADS_PER_STEP):
        s = lax.dot_general(qm_scr[h], kcat, NT_DIMS, preferred_element_type=F32)
        s = jnp.where(mask, s, NEG_INF)
        sink = sink_ref[0, h:h + 1, 0:1]
        m = jnp.maximum(jnp.max(s, axis=-1, keepdims=True), sink)
        p = jnp.exp(s - m)
        l = jnp.sum(p, axis=-1, keepdims=True) + jnp.exp(sink - m)
        acc_scr[h] = jnp.dot(p.astype(BF16), vcat, preferred_element_type=F32)
        inv_l.append(1.0 / l)
    _write_heads(o_ref, acc_scr, inv_l)


def swa_prompt(q, kb, vb, sink_perm):
    b, t, _ = q.shape
    w = WINDOW
    nb = t // w
    npair = N_KV // 2
    sink_arr = jnp.broadcast_to(sink_perm.astype(F32).reshape(npair, HEADS_PER_STEP, 1), (npair, HEADS_PER_STEP, LANES))
    qspec = pl.BlockSpec((1, w, SLABS_PER_STEP * LANES), lambda bi, j, i: (bi, i, j))
    cur = pl.BlockSpec((1, w, LANES), lambda bi, j, i: (bi, i, j))
    prev = pl.BlockSpec((1, w, LANES), lambda bi, j, i: (bi, jnp.maximum(i - 1, 0), j))
    return pl.pallas_call(
        _swa_prompt_kernel,
        grid=(b, npair, nb),
        in_specs=[qspec, prev, cur, prev, cur, pl.BlockSpec((1, HEADS_PER_STEP, LANES), lambda bi, j, i: (j, 0, 0))],
        out_specs=qspec,
        out_shape=jax.ShapeDtypeStruct(q.shape, BF16),
        scratch_shapes=[pltpu.VMEM((HEADS_PER_STEP, w, LANES), BF16), pltpu.VMEM((HEADS_PER_STEP, w, LANES), F32)],
        compiler_params=_cparams(("parallel", "parallel", "arbitrary")),
        name="swa_prompt",
    )(q, kb, kb, vb, vb, sink_arr)


def _cumsum_t_kernel(lf_ref, o_ref, carry_scr, *, rows):
    c = pl.program_id(1)

    @pl.when(c == 0)
    def _():
        carry_scr[...] = jnp.zeros(carry_scr.shape, F32)

    L = lf_ref.shape[1]
    tril_b = (_iota((L, L), 0) >= _iota((L, L), 1)).astype(BF16)
    cs = _dot01_left(tril_b, lf_ref[0]) + carry_scr[...]
    carry_scr[...] = cs[L - 1:L, :]
    o_ref[0] = cs.T[0:rows, :]


def cumsum_t(lf, rows):
    b, t, _ = lf.shape
    L = 128
    return pl.pallas_call(
        functools.partial(_cumsum_t_kernel, rows=rows),
        grid=(b, t // L),
        in_specs=[pl.BlockSpec((1, L, LANES), lambda i, j: (i, j, 0))],
        out_specs=pl.BlockSpec((1, rows, L), lambda i, j: (i, 0, j)),
        out_shape=jax.ShapeDtypeStruct((b, rows, t), F32),
        scratch_shapes=[pltpu.VMEM((1, LANES), F32)],
        compiler_params=_cparams(("parallel", "arbitrary")),
        name="cumsum_t",
    )(lf)


def _fox_prompt_kernel(qi_ref, ki_ref, q_ref, k_ref, v_ref, c_ref, o_ref, qm_scr, m_scr, l_scr, acc_scr, *, tq, tk):
    step = pl.program_id(2)
    qi = qi_ref[step]
    ki = ki_ref[step]
    last_k = (qi * tq + tq - 1) // tk

    @pl.when(ki == 0)
    def _():
        _masked_q(q_ref, qm_scr)
        m_scr[...] = jnp.full(m_scr.shape, NEG_INF, F32)
        l_scr[...] = jnp.zeros(l_scr.shape, F32)
        acc_scr[...] = jnp.zeros(acc_scr.shape, F32)

    def body(masked):
        kb = k_ref[0]
        vb = v_ref[0]
        if masked:
            row = qi * tq + _iota((tq, tk), 0)
            col = ki * tk + _iota((tq, tk), 1)
            keep = col <= row
        for h in range(HEADS_PER_STEP):
            s = lax.dot_general(qm_scr[h], kb, NT_DIMS, preferred_element_type=F32)
            s = s - c_ref[0, h:h + 1, :]
            if masked:
                s = jnp.where(keep, s, NEG_INF)
            m_prev = m_scr[h]
            m_new = jnp.maximum(m_prev, jnp.max(s, axis=-1, keepdims=True))
            alpha = jnp.exp(m_prev - m_new)
            p = jnp.exp(s - m_new)
            l_scr[h] = alpha * l_scr[h] + jnp.sum(p, axis=-1, keepdims=True)
            acc_scr[h] = alpha * acc_scr[h] + jnp.dot(p.astype(BF16), vb, preferred_element_type=F32)
            m_scr[h] = m_new

    needs_mask = (ki + 1) * tk - 1 > qi * tq

    @pl.when(needs_mask)
    def _():
        body(True)

    @pl.when(jnp.logical_not(needs_mask))
    def _():
        body(False)

    @pl.when(ki == last_k)
    def _():
        _write_heads(o_ref, acc_scr, [1.0 / l_scr[h] for h in range(HEADS_PER_STEP)])


def fox_prompt(q, kb, vb, cum_t, tq=256, tk=512):
    b, t, _ = q.shape
    tq, tk = min(tq, t), min(tk, t)
    npair = N_KV // 2
    qi_l, ki_l = [], []
    for qi in range(t // tq):
        for ki in range((qi * tq + tq - 1) // tk + 1):
            qi_l.append(qi)
            ki_l.append(ki)
    qi_arr = jnp.asarray(qi_l, jnp.int32)
    ki_arr = jnp.asarray(ki_l, jnp.int32)
    qspec = pl.BlockSpec((1, tq, SLABS_PER_STEP * LANES), lambda bi, j, s, qi, ki: (bi, qi[s], j))
    kspec = pl.BlockSpec((1, tk, LANES), lambda bi, j, s, qi, ki: (bi, ki[s], j))
    cspec = pl.BlockSpec((1, HEADS_PER_STEP, tk), lambda bi, j, s, qi, ki: (bi, j, ki[s]))
    grid_spec = pltpu.PrefetchScalarGridSpec(
        num_scalar_prefetch=2,
        grid=(b, npair, len(qi_l)),
        in_specs=[qspec, kspec, kspec, cspec],
        out_specs=qspec,
        scratch_shapes=[pltpu.VMEM((HEADS_PER_STEP, tq, LANES), BF16),
                        pltpu.VMEM((HEADS_PER_STEP, tq, 1), F32),
                        pltpu.VMEM((HEADS_PER_STEP, tq, 1), F32),
                        pltpu.VMEM((HEADS_PER_STEP, tq, LANES), F32)],
    )
    return pl.pallas_call(
        functools.partial(_fox_prompt_kernel, tq=tq, tk=tk),
        grid_spec=grid_spec,
        out_shape=jax.ShapeDtypeStruct(q.shape, BF16),
        compiler_params=_cparams(("parallel", "parallel", "arbitrary")),
        name="fox_prompt",
    )(qi_arr, ki_arr, q, kb, vb, cum_t)


def _xattn_kernel(q_ref, k_ref, v_ref, o_ref, *, nb):
    scale = XA_DIM ** -0.5
    for bi in range(nb):
        for h in range(XA_HEADS):
            sl = slice(h * XA_DIM, (h + 1) * XA_DIM)
            qh = q_ref[bi, :, sl]
            kh = k_ref[bi, :, sl].astype(BF16)
            vh = v_ref[bi, :, sl].astype(BF16)
            s = lax.dot_general(qh, kh, NT_DIMS, preferred_element_type=F32) * scale
            m = jnp.max(s, axis=-1, keepdims=True)
            p = jnp.exp(s - m)
            l = jnp.sum(p, axis=-1, keepdims=True)
            o = jnp.dot(p.astype(BF16), vh, preferred_element_type=F32) * (1.0 / l)
            o_ref[bi, :, sl] = o.astype(o_ref.dtype)


def xattn(q, mk, mv, tq, nb):
    b, t, _ = q.shape
    mem = mk.shape[1]
    qspec = pl.BlockSpec((nb, tq, XA_WIDTH), lambda i, j: (i, j, 0))
    mspec = pl.BlockSpec((nb, mem, XA_WIDTH), lambda i, j: (i, 0, 0))
    return pl.pallas_call(
        functools.partial(_xattn_kernel, nb=nb),
        grid=(b // nb, t // tq),
        in_specs=[qspec, mspec, mspec],
        out_specs=qspec,
        out_shape=jax.ShapeDtypeStruct(q.shape, BF16),
        compiler_params=_cparams(("parallel", "arbitrary")),
        name="xattn",
    )(q, mk, mv)


def _block_diag_q(q, nb, t):
    q4 = q.reshape(nb, t * N_HEADS, HEAD_DIM)
    qt = jnp.tile(q4, (1, 1, N_KV))
    row_kv = (np.arange(t * N_HEADS) % N_HEADS) // GROUP
    keep = jnp.asarray(row_kv[:, None] == (np.arange(N_KV * HEAD_DIM) // HEAD_DIM)[None, :])
    return jnp.where(keep[None], qt * jnp.asarray(HEAD_DIM ** -0.5, BF16), jnp.zeros((), BF16))


def _diag_extract(acc, rows):
    row_kv = (_iota((rows, HEAD_DIM), 0) % N_HEADS) // GROUP
    out = jnp.zeros((rows, HEAD_DIM), F32)
    for kv in range(N_KV):
        out = out + jnp.where(row_kv == kv, acc[:, kv * HEAD_DIM:(kv + 1) * HEAD_DIM], 0.0)
    return out


NEW_PAD = 16


def _swa_sample_kernel(q_ref, ck_ref, cv_ref, kn_ref, vn_ref, sink_ref, o_ref, ok_ref, ov_ref, *, nb, t):
    w = WINDOW
    rows = t * N_HEADS
    kvw = N_KV * HEAD_DIM
    tok = _iota((rows, 2 * w), 0) // N_HEADS
    col = _iota((rows, 2 * w), 1)
    mask = ((col > tok) & (col < w)) | ((col >= w) & (col - w <= tok))
    sink = sink_ref[:, 0:1]
    zpad = jnp.zeros((w - NEW_PAD, kvw), BF16)
    for bi in range(nb):
        ck = ck_ref[bi]
        cv = cv_ref[bi]
        kn = kn_ref[bi]
        vn = vn_ref[bi]
        kcat = jnp.concatenate([ck.astype(BF16), kn.astype(BF16), zpad], axis=0)
        vcat = jnp.concatenate([cv.astype(BF16), vn.astype(BF16), zpad], axis=0)
        s = lax.dot_general(q_ref[bi], kcat, NT_DIMS, preferred_element_type=F32)
        s = jnp.where(mask, s, NEG_INF)
        m = jnp.maximum(jnp.max(s, axis=-1, keepdims=True), sink)
        p = jnp.exp(s - m)
        l = jnp.sum(p, axis=-1, keepdims=True) + jnp.exp(sink - m)
        acc = jnp.dot(p.astype(BF16), vcat, preferred_element_type=F32)
        o_ref[bi] = (_diag_extract(acc, rows) * (1.0 / l)).astype(o_ref.dtype)
        ok_ref[bi, 0:w - t, :] = ck[t:w, :]
        ov_ref[bi, 0:w - t, :] = cv[t:w, :]
        ok_ref[bi, w - t:w, :] = kn[0:t, :]
        ov_ref[bi, w - t:w, :] = vn[0:t, :]


def swa_sample(qbd, ck, cv, kn, vn, sink, t, nb=8):
    b = qbd.shape[0]
    w = WINDOW
    rows = t * N_HEADS
    sink_rows = jnp.broadcast_to(jnp.tile(sink.astype(F32), t).reshape(rows, 1), (rows, LANES))
    bspec = lambda s1, s2: pl.BlockSpec((nb, s1, s2), lambda i: (i, 0, 0))
    kvw = N_KV * HEAD_DIM
    return pl.pallas_call(
        functools.partial(_swa_sample_kernel, nb=nb, t=t),
        grid=(b // nb,),
        in_specs=[bspec(rows, kvw), bspec(w, kvw), bspec(w, kvw), bspec(NEW_PAD, kvw), bspec(NEW_PAD, kvw),
                  _const_spec((rows, LANES))],
        out_specs=[bspec(rows, HEAD_DIM), bspec(w, kvw), bspec(w, kvw)],
        out_shape=[jax.ShapeDtypeStruct((b, rows, HEAD_DIM), BF16),
                   jax.ShapeDtypeStruct((b, w, kvw), F32), jax.ShapeDtypeStruct((b, w, kvw), F32)],
        compiler_params=_cparams(("parallel",)),
        name="swa_sample",
    )(qbd, ck, cv, kn, vn, sink_rows)


FOX_PAGES_PER_STEP = 16


def _fox_sample_kernel(pt_ref, q_ref, kn_ref, vn_ref, lfn_ref, kc_hbm, vc_hbm, lf_hbm, o_ref,
                       kbuf, vbuf, lbuf, sem, carry_scr, m_scr, l_scr, acc_scr,
                       *, t, n_pages, pg):
    b = pl.program_id(0)
    c = pl.program_id(1)
    nch = pl.num_programs(1)
    step = b * nch + c
    total = pl.num_programs(0) * nch
    rows = t * N_HEADS
    nh = N_HEADS

    def copies(stp, slot):
        base = (stp // nch) * n_pages + (stp % nch) * pg
        out = []
        for p in range(pg):
            page = pt_ref[base + p]
            out.append(pltpu.make_async_copy(kc_hbm.at[page], kbuf.at[slot, pl.ds(p * PAGE, PAGE)], sem.at[slot, 0]))
            out.append(pltpu.make_async_copy(vc_hbm.at[page], vbuf.at[slot, pl.ds(p * PAGE, PAGE)], sem.at[slot, 1]))
            out.append(pltpu.make_async_copy(lf_hbm.at[page], lbuf.at[slot, pl.ds(p * nh, nh)], sem.at[slot, 2]))
        return out

    slot = step % 2

    @pl.when(step == 0)
    def _():
        for cp in copies(step, slot):
            cp.start()

    @pl.when(step + 1 < total)
    def _():
        for cp in copies(step + 1, 1 - slot):
            cp.start()

    @pl.when(c == 0)
    def _():
        carry_scr[...] = jnp.zeros(carry_scr.shape, F32)
        m_scr[...] = jnp.full(m_scr.shape, NEG_INF, F32)
        l_scr[...] = jnp.zeros(l_scr.shape, F32)
        acc_scr[...] = jnp.zeros(acc_scr.shape, F32)

    for cp in copies(step, slot):
        cp.wait()

    q = q_ref[0]
    upper_b = (_iota((PAGE, PAGE), 0) <= _iota((PAGE, PAGE), 1)).astype(BF16)

    def online(s, vals):
        m_prev = m_scr[...]
        m_new = jnp.maximum(m_prev, jnp.max(s, axis=-1, keepdims=True))
        alpha = jnp.exp(m_prev - m_new)
        p = jnp.exp(s - m_new)
        l_scr[...] = alpha * l_scr[...] + jnp.sum(p, axis=-1, keepdims=True)
        acc_scr[...] = alpha * acc_scr[...] + jnp.dot(p.astype(BF16), vals, preferred_element_type=F32)
        m_scr[...] = m_new

    lf = lbuf[slot]
    cl = _dot01_right(lf, upper_b)
    tot = jnp.broadcast_to(cl[:, PAGE - 1:PAGE], cl.shape)
    r = _iota((pg * nh, pg * nh), 0)
    cc = _iota((pg * nh, pg * nh), 1)
    earlier_b = ((r % nh == cc % nh) & (cc // nh < r // nh)).astype(BF16)
    cum = cl + _dot01_left(earlier_b, tot) + jnp.concatenate([carry_scr[...]] * pg, axis=0)
    carry_new = jnp.broadcast_to(cum[(pg - 1) * nh:pg * nh, PAGE - 1:PAGE], (nh, LANES))
    carry_scr[...] = carry_new
    bias16 = jnp.concatenate([cum[p * nh:(p + 1) * nh, :] for p in range(pg)], axis=1)
    bias = jnp.concatenate([bias16] * t, axis=0)

    kc = kbuf[slot].astype(BF16)
    vc = vbuf[slot].astype(BF16)
    s = lax.dot_general(q, kc, NT_DIMS, preferred_element_type=F32) - bias
    online(s, vc)

    @pl.when(c == nch - 1)
    def _():
        zpad = jnp.zeros((PAGE - NEW_PAD, N_KV * HEAD_DIM), BF16)
        knew = jnp.concatenate([kn_ref[0].astype(BF16), zpad], axis=0)
        vnew = jnp.concatenate([vn_ref[0].astype(BF16), zpad], axis=0)
        cum_n = _dot01_right(lfn_ref[0], upper_b) + carry_new
        sn = lax.dot_general(q, knew, NT_DIMS, preferred_element_type=F32)
        sn = sn - jnp.concatenate([cum_n] * t, axis=0)
        tok = _iota((rows, PAGE), 0) // nh
        sn = jnp.where(_iota((rows, PAGE), 1) <= tok, sn, NEG_INF)
        online(sn, vnew)
        o_ref[0] = (_diag_extract(acc_scr[...], rows) * (1.0 / l_scr[...])).astype(o_ref.dtype)


def fox_sample(page_table, qbd, kn, vn, lfn_t, kcache, vcache, lf_t_cache, t, pg=FOX_PAGES_PER_STEP):
    b, n_pages = page_table.shape
    pg = min(pg, n_pages)
    assert n_pages % pg == 0
    rows = t * N_HEADS
    kvw = N_KV * HEAD_DIM
    bspec = lambda s1, s2: pl.BlockSpec((1, s1, s2), lambda i, j, pt: (i, 0, 0))
    any_spec = pl.BlockSpec(memory_space=pl.ANY)
    grid_spec = pltpu.PrefetchScalarGridSpec(
        num_scalar_prefetch=1,
        grid=(b, n_pages // pg),
        in_specs=[bspec(rows, kvw), bspec(NEW_PAD, kvw), bspec(NEW_PAD, kvw), bspec(N_HEADS, LANES),
                  any_spec, any_spec, any_spec],
        out_specs=bspec(rows, HEAD_DIM),
        scratch_shapes=[pltpu.VMEM((2, pg * PAGE, kvw), F32), pltpu.VMEM((2, pg * PAGE, kvw), F32),
                        pltpu.VMEM((2, pg * N_HEADS, LANES), F32), pltpu.SemaphoreType.DMA((2, 3)),
                        pltpu.VMEM((N_HEADS, LANES), F32),
                        pltpu.VMEM((rows, 1), F32), pltpu.VMEM((rows, 1), F32), pltpu.VMEM((rows, kvw), F32)],
    )
    return pl.pallas_call(
        functools.partial(_fox_sample_kernel, t=t, n_pages=n_pages, pg=pg),
        grid_spec=grid_spec,
        out_shape=jax.ShapeDtypeStruct((b, rows, HEAD_DIM), BF16),
        compiler_params=_cparams(("arbitrary", "arbitrary")),
        name="fox_sample",
    )(page_table.reshape(-1), qbd, kn, vn, lfn_t, kcache, vcache, lf_t_cache)


def _pad_cols(w, width):
    return jnp.pad(w, ((0, 0), (0, width - w.shape[1])))


def kernel(x_prompt, x_sample, state_ssm, state_conv, cache_swa_k, cache_swa_v, cache_fox_k, cache_fox_v,
           cache_fox_logf, cache_mem_k, cache_mem_v, page_table, mem_prompt, norm_mix, norm_xa, norm_mem,
           norm_ffn, w_in_even, conv_w, conv_b, dt_bias, a_log, d_skip, ssm_norm, swa_sink, w_out_even,
           w_in_odd, fox_fb, w_out_odd, w_xq, w_xk, w_xv, w_xo, w_ffn_in, w_ffn_out, norm_final):
    bp, seq, d = x_prompt.shape
    bs, ts, _ = x_sample.shape
    depth = norm_mix.shape[0]
    mem_len = mem_prompt.shape[1]
    kvw = N_KV * HEAD_DIM
    qw = N_HEADS * HEAD_DIM
    xp = x_prompt.reshape(bp * seq, d)
    xs = x_sample.reshape(bs * ts, d)
    mem2 = mem_prompt.reshape(bp * mem_len, d)
    outs = {k: [] for k in ("p_ssm", "p_conv", "p_swk", "p_swv", "p_fk", "p_fv", "p_fl", "p_mk", "p_mv",
                            "s_ssm", "s_conv", "s_swk", "s_swv", "s_fk", "s_fv", "s_fl")}

    for l in range(depth):
        li = l // 2
        if l % 2 == 0:
            w = w_in_even[li]
            o_z, o_xbc, o_dt = 0, SSM_WIDTH, SSM_WIDTH + CONV_DIM
            o_q = o_dt + SSM_HEADS
            o_k, o_v = o_q + qw, o_q + qw + kvw
            wz, wxbc, wdt = w[:, o_z:o_xbc], w[:, o_xbc:o_dt], _pad_cols(w[:, o_dt:o_q], LANES)
            wq, wk, wv = w[:, o_q:o_k], w[:, o_k:o_v], w[:, o_v:o_v + kvw]
            c_xbc, c_q, c_k, c_v, c_dt = SSM_WIDTH, SSM_WIDTH + CONV_DIM, SSM_WIDTH + CONV_DIM + qw, \
                SSM_WIDTH + CONV_DIM + qw + kvw, SSM_WIDTH + CONV_DIM + qw + 2 * kvw
            n_tot = c_dt + LANES
            wo = w_out_even[li]
            wo_ssm = wo[:SSM_WIDTH].astype(BF16)
            wo_att = wo[SSM_WIDTH:]
            ssm_params = (conv_w[li], conv_b[li], dt_bias[li], a_log[li], d_skip[li], ssm_norm[li])

            w_p = jnp.concatenate([wz, wxbc, wq[:, COL_PERM], wk, wv, wdt], axis=1).astype(BF16)
            segs_p = [(0, c_xbc, (F32,), None), (c_xbc, c_q, (F32,), None), (c_q, c_k, (BF16,), None),
                      (c_k, c_v, (F32, BF16), None), (c_v, c_dt, (F32, BF16), None), (c_dt, n_tot, (F32,), None)]
            z, xbc, q, k, kb, v, vb, dtr = norm_proj(xp, norm_mix[l], w_p, segs_p)
            xbc3 = xbc.reshape(bp, seq, CONV_DIM)
            y_ssm, h = ssd(xbc3, z.reshape(bp, seq, SSM_WIDTH), dtr.reshape(bp, seq, LANES),
                           jnp.zeros((bp, 8, CONV_DIM), F32), None, *ssm_params, L=SSM_CHUNK, valid=SSM_CHUNK)
            y_att = swa_prompt(q.reshape(bp, seq, qw), kb.reshape(bp, seq, kvw), vb.reshape(bp, seq, kvw),
                               swa_sink[li][HEAD_PERM])
            xp = matmul_residual([y_ssm.reshape(bp * seq, SSM_WIDTH), y_att.reshape(bp * seq, qw)],
                                 [wo_ssm, wo_att[COL_PERM].astype(BF16)], xp)
            outs["p_ssm"].append(h.reshape(bp, SSM_HEADS, SSM_P, SSM_N))
            outs["p_conv"].append(xbc3[:, seq - (CONV_K - 1):])
            outs["p_swk"].append(k.reshape(bp, seq, N_KV, HEAD_DIM)[:, seq - WINDOW:])
            outs["p_swv"].append(v.reshape(bp, seq, N_KV, HEAD_DIM)[:, seq - WINDOW:])

            w_s = jnp.concatenate([wz, wxbc, wq, wk, wv, wdt], axis=1).astype(BF16)
            segs_s = [(0, c_xbc, (F32,), None), (c_xbc, c_q, (F32,), None), (c_q, c_k, (BF16,), None),
                      (c_k, c_v, (F32,), None), (c_v, c_dt, (F32,), None), (c_dt, n_tot, (F32,), None)]
            z, xbc, q, k, v, dtr = norm_proj(xs, norm_mix[l], w_s, segs_s)
            ls = NEW_PAD
            pad_t = lambda a: jnp.pad(a.reshape(bs, ts, a.shape[-1]), ((0, 0), (0, ls - ts), (0, 0)))
            xbc3 = xbc.reshape(bs, ts, CONV_DIM)
            tail = jnp.pad(state_conv[li].astype(F32), ((0, 0), (8 - (CONV_K - 1), 0), (0, 0)))
            y_ssm, h = ssd(pad_t(xbc), pad_t(z), pad_t(dtr), tail, state_ssm[li].reshape(bs, SSM_WIDTH, SSM_N),
                           *ssm_params, L=ls, valid=ts)
            y_ssm = y_ssm[:, :ts].reshape(bs * ts, SSM_WIDTH)
            kn, vn = pad_t(k), pad_t(v)
            o_att, new_k, new_v = swa_sample(_block_diag_q(q, bs, ts), cache_swa_k[li].reshape(bs, WINDOW, kvw),
                                             cache_swa_v[li].reshape(bs, WINDOW, kvw), kn, vn, swa_sink[li], ts)
            xs = matmul_residual([y_ssm, o_att.reshape(bs * ts, qw)], [wo_ssm, wo_att.astype(BF16)], xs)
            outs["s_ssm"].append(h.reshape(bs, SSM_HEADS, SSM_P, SSM_N))
            outs["s_conv"].append(jnp.concatenate([state_conv[li].astype(F32), xbc3], axis=1)[:, -(CONV_K - 1):])
            outs["s_swk"].append(new_k.reshape(bs, WINDOW, N_KV, HEAD_DIM))
            outs["s_swv"].append(new_v.reshape(bs, WINDOW, N_KV, HEAD_DIM))
        else:
            w = w_in_odd[li]
            wq, wk, wv, wf = w[:, :qw], w[:, qw:qw + kvw], w[:, qw + kvw:qw + 2 * kvw], w[:, qw + 2 * kvw:]
            c_k, c_v, c_f = qw, qw + kvw, qw + 2 * kvw
            n_tot = c_f + LANES
            fb = fox_fb[li].astype(F32)

            wf_p = _pad_cols(jnp.concatenate([wf[:, HEAD_PERM], wf], axis=1), LANES)
            fb_p = jnp.pad(jnp.concatenate([fb[HEAD_PERM], fb]), (0, LANES - 2 * N_HEADS)).reshape(1, LANES)
            w_p = jnp.concatenate([wq[:, COL_PERM], wk, wv, wf_p], axis=1).astype(BF16)
            segs_p = [(0, c_k, (BF16,), None), (c_k, c_v, (F32, BF16), None), (c_v, c_f, (F32, BF16), None),
                      (c_f, n_tot, (F32,), "logsig")]
            q, k, kb, v, vb, lf = norm_proj(xp, norm_mix[l], w_p, segs_p, aux=fb_p)
            lf3 = lf.reshape(bp, seq, LANES)
            cum_t = cumsum_t(lf3, N_HEADS)
            y = fox_prompt(q.reshape(bp, seq, qw), kb.reshape(bp, seq, kvw), vb.reshape(bp, seq, kvw), cum_t)
            xp = matmul_residual([y.reshape(bp * seq, qw)], [w_out_odd[li][COL_PERM].astype(BF16)], xp)
            outs["p_fk"].append(k.reshape(bp, seq, N_KV, HEAD_DIM))
            outs["p_fv"].append(v.reshape(bp, seq, N_KV, HEAD_DIM))
            outs["p_fl"].append(lf3[:, :, N_HEADS:2 * N_HEADS])

            w_s = jnp.concatenate([wq, wk, wv, _pad_cols(wf, LANES)], axis=1).astype(BF16)
            fb_s = jnp.pad(fb, (0, LANES - N_HEADS)).reshape(1, LANES)
            segs_s = [(0, c_k, (BF16,), None), (c_k, c_v, (F32,), None), (c_v, c_f, (F32,), None),
                      (c_f, n_tot, (F32,), "logsig")]
            q, k, v, lf = norm_proj(xs, norm_mix[l], w_s, segs_s, aux=fb_s)
            lf_new = lf[:, :N_HEADS].reshape(bs, ts, N_HEADS)
            pad_t = lambda a: jnp.pad(a.reshape(bs, ts, a.shape[-1]), ((0, 0), (0, NEW_PAD - ts), (0, 0)))
            lfn_t = jnp.pad(jnp.swapaxes(lf_new, 1, 2), ((0, 0), (0, 0), (0, LANES - ts)))
            pool = cache_fox_k.shape[1]
            y = fox_sample(page_table, _block_diag_q(q, bs, ts), pad_t(k), pad_t(v), lfn_t,
                           cache_fox_k[li].reshape(pool, PAGE, kvw), cache_fox_v[li].reshape(pool, PAGE, kvw),
                           jnp.swapaxes(cache_fox_logf[li].astype(F32), 1, 2), ts)
            xs = matmul_residual([y.reshape(bs * ts, qw)], [w_out_odd[li].astype(BF16)], xs)
            outs["s_fk"].append(k.reshape(bs, ts, N_KV, HEAD_DIM))
            outs["s_fv"].append(v.reshape(bs, ts, N_KV, HEAD_DIM))
            outs["s_fl"].append(lf_new)

        wkv = jnp.concatenate([w_xk[l], w_xv[l]], axis=1).astype(BF16)
        mk, mv = norm_proj(mem2, norm_mem[l], wkv, [(0, XA_WIDTH, (F32,), None), (XA_WIDTH, 2 * XA_WIDTH, (F32,), None)])
        wq_b = w_xq[l].astype(BF16)
        wo_b = w_xo[l].astype(BF16)
        (qx,) = norm_proj(xp, norm_xa[l], wq_b, [(0, XA_WIDTH, (BF16,), None)])
        ox = xattn(qx.reshape(bp, seq, XA_WIDTH), mk.reshape(bp, mem_len, XA_WIDTH), mv.reshape(bp, mem_len, XA_WIDTH),
                   tq=512, nb=1)
        xp = matmul_residual([ox.reshape(bp * seq, XA_WIDTH)], [wo_b], xp)
        (qx,) = norm_proj(xs, norm_xa[l], wq_b, [(0, XA_WIDTH, (BF16,), None)])
        qx = jnp.pad(qx.reshape(bs, ts, XA_WIDTH), ((0, 0), (0, NEW_PAD - ts), (0, 0)))
        ox = xattn(qx, cache_mem_k[l].reshape(bs, mem_len, XA_WIDTH), cache_mem_v[l].reshape(bs, mem_len, XA_WIDTH),
                   tq=NEW_PAD, nb=4)
        xs = matmul_residual([ox[:, :ts].reshape(bs * ts, XA_WIDTH)], [wo_b], xs)
        outs["p_mk"].append(mk.reshape(bp, mem_len, XA_HEADS, XA_DIM))
        outs["p_mv"].append(mv.reshape(bp, mem_len, XA_HEADS, XA_DIM))

        wg = w_ffn_in[l][:, :FFN_H].astype(BF16)
        wu = w_ffn_in[l][:, FFN_H:].astype(BF16)
        wd = w_ffn_out[l].astype(BF16)
        gf = norm_final if l == depth - 1 else None
        xp = ffn(xp, norm_ffn[l], wg, wu, wd, g_final=gf)
        xs = ffn(xs, norm_ffn[l], wg, wu, wd, g_final=gf)

    st = lambda key: jnp.stack(outs[key])
    return (xp.reshape(bp, seq, d), xs.reshape(bs, ts, d),
            st("p_ssm"), st("p_conv"), st("p_swk"), st("p_swv"), st("p_fk"), st("p_fv"), st("p_fl"),
            st("p_mk"), st("p_mv"), st("s_ssm"), st("s_conv"), st("s_swk"), st("s_swv"),
            st("s_fk"), st("s_fv"), st("s_fl"))
```

```python
import functools
import math

import numpy as np
import jax
import jax.numpy as jnp
from jax import lax
from jax.experimental import pallas as pl
from jax.experimental.pallas import tpu as pltpu

F32 = jnp.float32
BF16 = jnp.bfloat16

D_MODEL = 1024
HEAD_DIM = 64
N_HEADS = 16
N_KV = 4
SSM_HEADS = 16
SSM_P = 64
SSM_WIDTH = 1024
SSM_N = 128
SSM_GROUPS = 2
SSM_CHUNK = 128
CONV_K = 4
CONV_DIM = 1536
WINDOW = 128
PAGE = 128
XA_HEADS = 4
XA_DIM = 128
XA_WIDTH = 512
FFN_H = 2816
RMS_EPS = 1e-6
NEG_INF = -1e30

V7X_VMEM_BYTES = 64 * 1024 * 1024
VMEM_LIMIT = V7X_VMEM_BYTES - 8 * 1024 * 1024
LANES = 128

NT_DIMS = (((1,), (1,)), ((), ()))


def _cparams(sem):
    return pltpu.CompilerParams(dimension_semantics=sem, vmem_limit_bytes=VMEM_LIMIT)


def _const_spec(shape):
    nd = len(shape)
    return pl.BlockSpec(shape, lambda *_: (0,) * nd, pipeline_mode=pl.Buffered(1))


def _softplus(x):
    return jnp.maximum(x, 0.0) + jnp.log1p(jnp.exp(-jnp.abs(x)))


def _silu(x):
    return x * (1.0 / (1.0 + jnp.exp(-x)))


def _split3(x):
    hi = x.astype(BF16)
    r1 = x - hi.astype(F32)
    mid = r1.astype(BF16)
    lo = (r1 - mid.astype(F32)).astype(BF16)
    return hi, mid, lo


def _dot01_left(m01, x):
    hi, mid, lo = _split3(x)
    d = lambda b: jnp.dot(m01, b, preferred_element_type=F32)
    return (d(lo) + d(mid)) + d(hi)


def _dot01_right(x, m01):
    hi, mid, lo = _split3(x)
    d = lambda a: jnp.dot(a, m01, preferred_element_type=F32)
    return (d(lo) + d(mid)) + d(hi)


def _iota(shape, dim):
    return lax.broadcasted_iota(jnp.int32, shape, dim)


def _norm_proj_kernel(*refs, segs, has_aux):
    x_ref, g_ref, w_ref = refs[:3]
    pos = 3
    aux_ref = None
    if has_aux:
        aux_ref = refs[3]
        pos = 4
    out_refs = refs[pos:]
    x = x_ref[...]
    ms = jnp.mean(x * x, axis=-1, keepdims=True)
    xn = (x * lax.rsqrt(ms + RMS_EPS)) * g_ref[...]
    xb = xn.astype(BF16)
    oi = 0
    for (c0, c1, dtypes, act) in segs:
        y = jnp.dot(xb, w_ref[:, c0:c1], preferred_element_type=F32)
        if act == "logsig":
            y = -_softplus(-(y + aux_ref[...]))
        elif act is not None:
            y = y * act
        for dt in dtypes:
            out_refs[oi][...] = y.astype(dt)
            oi += 1


def norm_proj(x, g, w, segs, aux=None, tm=512):
    m, d = x.shape
    tm = min(tm, m)
    assert m % tm == 0
    n = w.shape[1]
    in_specs = [pl.BlockSpec((tm, d), lambda i: (i, 0)), _const_spec((1, d)), _const_spec((d, n))]
    args = [x, g.reshape(1, d).astype(F32), w]
    if aux is not None:
        in_specs.append(_const_spec(aux.shape))
        args.append(aux)
    out_shape, out_specs = [], []
    for (c0, c1, dtypes, _) in segs:
        for dt in dtypes:
            out_shape.append(jax.ShapeDtypeStruct((m, c1 - c0), dt))
            out_specs.append(pl.BlockSpec((tm, c1 - c0), lambda i: (i, 0)))
    return pl.pallas_call(
        functools.partial(_norm_proj_kernel, segs=tuple(segs), has_aux=aux is not None),
        grid=(m // tm,),
        in_specs=in_specs,
        out_specs=out_specs,
        out_shape=out_shape,
        compiler_params=_cparams(("parallel",)),
        name="norm_proj",
    )(*args)


def _mm_res_kernel(*refs, n):
    a_refs, w_refs = refs[:n], refs[n:2 * n]
    res_ref, o_ref = refs[2 * n], refs[2 * n + 1]
    acc = res_ref[...]
    for a_ref, w_ref in zip(a_refs, w_refs):
        acc = acc + jnp.dot(a_ref[...].astype(BF16), w_ref[...], preferred_element_type=F32)
    o_ref[...] = acc


def matmul_residual(a_list, w_list, res, tm=512):
    m, d = res.shape
    tm = min(tm, m)
    assert m % tm == 0
    n = len(a_list)
    in_specs = [pl.BlockSpec((tm, a.shape[1]), lambda i: (i, 0)) for a in a_list]
    in_specs += [_const_spec(w.shape) for w in w_list]
    in_specs.append(pl.BlockSpec((tm, d), lambda i: (i, 0)))
    return pl.pallas_call(
        functools.partial(_mm_res_kernel, n=n),
        grid=(m // tm,),
        in_specs=in_specs,
        out_specs=pl.BlockSpec((tm, d), lambda i: (i, 0)),
        out_shape=jax.ShapeDtypeStruct((m, d), F32),
        compiler_params=_cparams(("parallel",)),
        name="matmul_residual",
    )(*a_list, *w_list, res)


FFN_CHUNK = 512


def _ffn_kernel(*refs, final_norm):
    if final_norm:
        x_ref, g_ref, wg_ref, wu_ref, wo_ref, gf_ref, o_ref, acc_ref = refs
    else:
        x_ref, g_ref, wg_ref, wu_ref, wo_ref, o_ref, acc_ref = refs
    x = x_ref[...]
    ms = jnp.mean(x * x, axis=-1, keepdims=True)
    xb = ((x * lax.rsqrt(ms + RMS_EPS)) * g_ref[...]).astype(BF16)
    hdim = wg_ref.shape[1]
    acc_ref[...] = x
    for c0 in range(0, hdim, FFN_CHUNK):
        c1 = min(c0 + FFN_CHUNK, hdim)
        gt = jnp.dot(xb, wg_ref[:, c0:c1], preferred_element_type=F32)
        up = jnp.dot(xb, wu_ref[:, c0:c1], preferred_element_type=F32)
        act = (_silu(gt) * up).astype(BF16)
        acc_ref[...] += jnp.dot(act, wo_ref[c0:c1, :], preferred_element_type=F32)
    y = acc_ref[...]
    if final_norm:
        ms2 = jnp.mean(y * y, axis=-1, keepdims=True)
        y = (y * lax.rsqrt(ms2 + RMS_EPS)) * gf_ref[...]
    o_ref[...] = y


def ffn(x, g, wg, wu, wo, g_final=None, tm=512):
    m, d = x.shape
    tm = min(tm, m)
    assert m % tm == 0
    final_norm = g_final is not None
    in_specs = [pl.BlockSpec((tm, d), lambda i: (i, 0)), _const_spec((1, d)),
                _const_spec(wg.shape), _const_spec(wu.shape), _const_spec(wo.shape)]
    args = [x, g.reshape(1, d).astype(F32), wg, wu, wo]
    if final_norm:
        in_specs.append(_const_spec((1, d)))
        args.append(g_final.reshape(1, d).astype(F32))
    return pl.pallas_call(
        functools.partial(_ffn_kernel, final_norm=final_norm),
        grid=(m // tm,),
        in_specs=in_specs,
        out_specs=pl.BlockSpec((tm, d), lambda i: (i, 0)),
        out_shape=jax.ShapeDtypeStruct((m, d), F32),
        scratch_shapes=[pltpu.VMEM((tm, d), F32)],
        compiler_params=_cparams(("parallel",)),
        name="ffn",
    )(*args)


def _ssd_kernel(*refs, L, valid, has_h0):
    if has_h0:
        (xbc_ref, z_ref, dtr_ref, tail_ref, h0_ref, cw_ref, cb_ref, dtb_ref, al_ref, ale_ref,
         dsk_ref, gn_ref, y_ref, hout_ref, xpad_scr, hT_scr) = refs
    else:
        (xbc_ref, z_ref, dtr_ref, tail_ref, cw_ref, cb_ref, dtb_ref, al_ref, ale_ref,
         dsk_ref, gn_ref, y_ref, hout_ref, xpad_scr, hT_scr) = refs
    c = pl.program_id(1)
    nc = pl.num_programs(1)

    @pl.when(c == 0)
    def _():
        xpad_scr[0:8, :] = tail_ref[0]
        if has_h0:
            hT_scr[...] = h0_ref[0].T
        else:
            hT_scr[...] = jnp.zeros(hT_scr.shape, F32)

    @pl.when(c > 0)
    def _():
        xpad_scr[0:8, :] = xpad_scr[L:L + 8, :]

    xpad_scr[8:8 + L, :] = xbc_ref[0]
    acc = cb_ref[...]
    for j in range(CONV_K):
        acc = acc + xpad_scr[5 + j:5 + j + L, :] * cw_ref[j:j + 1, :]
    xc = _silu(acc)
    xs = xc[:, :SSM_WIDTH]
    gn_w = SSM_GROUPS * SSM_N
    bm = xc[:, SSM_WIDTH:SSM_WIDTH + gn_w]
    cm = xc[:, SSM_WIDTH + gn_w:]

    dt = _softplus(dtr_ref[0] + dtb_ref[...])
    if valid < L:
        dt = jnp.where(_iota((L, LANES), 0) < valid, dt, 0.0)
    a128 = -jnp.exp(al_ref[...])
    a_exp = -jnp.exp(ale_ref[...])

    expand = (_iota((LANES, SSM_WIDTH), 1) // SSM_P == _iota((LANES, SSM_WIDTH), 0)).astype(BF16)
    tril = (_iota((L, L), 0) >= _iota((L, L), 1))
    tril_b = tril.astype(BF16)

    dt_exp = _dot01_right(dt, expand)
    cs_exp = _dot01_left(tril_b, dt_exp * a_exp)
    cs = _dot01_left(tril_b, dt * a128)
    cs_t = cs.T

    last = cs_exp[L - 1:L, :]
    ecs = jnp.exp(cs_exp)
    wend = jnp.exp(last - cs_exp)
    dec = jnp.exp(last)
    xdt = xs * dt_exp
    xdt_b = xdt.astype(BF16)
    xw_b = (xdt * wend).astype(BF16)

    lane_lo = _iota((L, LANES), 1) < SSM_P
    gw = SSM_WIDTH // SSM_GROUPS
    y_parts = []
    for g in range(SSM_GROUPS):
        bg = bm[:, g * SSM_N:(g + 1) * SSM_N]
        cg_b = cm[:, g * SSM_N:(g + 1) * SSM_N].astype(BF16)
        cb = lax.dot_general(cg_b, bg.astype(BF16), NT_DIMS, preferred_element_type=F32)
        h_old = hT_scr[:, g * gw:(g + 1) * gw]
        y_state = jnp.dot(cg_b, h_old.astype(BF16), preferred_element_type=F32)
        pairs = []
        for i in range(gw // LANES):
            slab = g * (gw // LANES) + i
            xp = xdt_b[:, slab * LANES:(slab + 1) * LANES]
            halves = []
            for half in range(2):
                h = 2 * slab + half
                diff = cs[:, h:h + 1] - cs_t[h:h + 1, :]
                dm = jnp.exp(jnp.where(tril, diff, NEG_INF))
                sc = (cb * dm).astype(BF16)
                halves.append(jnp.dot(sc, xp, preferred_element_type=F32))
            pairs.append(jnp.where(lane_lo, halves[0], halves[1]))
        y_intra = jnp.concatenate(pairs, axis=1)
        y_parts.append(y_intra + y_state * ecs[:, g * gw:(g + 1) * gw])
        bg_t = bg.T.astype(BF16)
        upd = jnp.dot(bg_t, xw_b[:, g * gw:(g + 1) * gw], preferred_element_type=F32)
        hT_scr[:, g * gw:(g + 1) * gw] = h_old * dec[:, g * gw:(g + 1) * gw] + upd
    y = jnp.concatenate(y_parts, axis=1) + dsk_ref[...] * xs
    yg = y * _silu(z_ref[0])
    ms = jnp.mean(yg * yg, axis=-1, keepdims=True)
    y_ref[0] = ((yg * lax.rsqrt(ms + RMS_EPS)) * gn_ref[...]).astype(y_ref.dtype)

    @pl.when(c == nc - 1)
    def _():
        hout_ref[0] = hT_scr[...].T


def ssd(xbc, z, dtr, tail, h0, conv_w, conv_b, dt_bias, a_log, d_skip, ssm_norm, L, valid):
    b, t, _ = xbc.shape
    assert t % L == 0
    nc = t // L
    has_h0 = h0 is not None
    pad16 = lambda v: jnp.pad(v.astype(F32), (0, LANES - v.shape[0])).reshape(1, LANES)
    rep = lambda v: jnp.repeat(v.astype(F32), SSM_P).reshape(1, SSM_WIDTH)
    row = lambda last: pl.BlockSpec((1, L, last), lambda i, j: (i, j, 0))
    per_b = lambda s1, s2: pl.BlockSpec((1, s1, s2), lambda i, j: (i, 0, 0))
    in_specs = [row(CONV_DIM), row(SSM_WIDTH), row(LANES), per_b(8, CONV_DIM)]
    args = [xbc, z, dtr, tail]
    if has_h0:
        in_specs.append(per_b(SSM_WIDTH, SSM_N))
        args.append(h0)
    consts = [conv_w.astype(F32), conv_b.reshape(1, CONV_DIM).astype(F32), pad16(dt_bias), pad16(a_log),
              rep(a_log), rep(d_skip), ssm_norm.reshape(1, SSM_WIDTH).astype(F32)]
    in_specs += [_const_spec(cst.shape) for cst in consts]
    args += consts
    return pl.pallas_call(
        functools.partial(_ssd_kernel, L=L, valid=valid, has_h0=has_h0),
        grid=(b, nc),
        in_specs=in_specs,
        out_specs=[row(SSM_WIDTH), per_b(SSM_WIDTH, SSM_N)],
        out_shape=[jax.ShapeDtypeStruct((b, t, SSM_WIDTH), BF16),
                   jax.ShapeDtypeStruct((b, SSM_WIDTH, SSM_N), F32)],
        scratch_shapes=[pltpu.VMEM((L + 8, CONV_DIM), F32), pltpu.VMEM((SSM_N, SSM_WIDTH), F32)],
        compiler_params=_cparams(("parallel", "arbitrary")),
        name="ssd",
    )(*args)


GROUP = N_HEADS // N_KV


def _head_perm():
    order = []
    for slab in range(N_HEADS // 2):
        j, g = slab // GROUP, slab % GROUP
        order += [(2 * j) * GROUP + g, (2 * j + 1) * GROUP + g]
    return np.asarray(order)


HEAD_PERM = _head_perm()
COL_PERM = (HEAD_PERM[:, None] * HEAD_DIM + np.arange(HEAD_DIM)[None, :]).reshape(-1)
SLABS_PER_STEP = 4
HEADS_PER_STEP = 2 * SLABS_PER_STEP


LOG2E = math.log2(math.e)
Q_SCALE = LOG2E * HEAD_DIM ** -0.5


def _masked_q(q_ref, qm_scr):
    tq = q_ref.shape[1]
    lo = _iota((tq, LANES), 1) < HEAD_DIM
    for s in range(SLABS_PER_STEP):
        slab = q_ref[0, :, s * LANES:(s + 1) * LANES]
        zero = jnp.zeros_like(slab)
        qm_scr[2 * s] = jnp.where(lo, slab, zero)
        qm_scr[2 * s + 1] = jnp.where(lo, zero, slab)


def _write_heads(o_ref, acc, inv_l):
    tq = o_ref.shape[1]
    lo = _iota((tq, LANES), 1) < HEAD_DIM
    for s in range(SLABS_PER_STEP):
        a = acc[2 * s] * inv_l[2 * s]
        b = acc[2 * s + 1] * inv_l[2 * s + 1]
        o_ref[0, :, s * LANES:(s + 1) * LANES] = jnp.where(lo, a, b).astype(o_ref.dtype)


def _swa_prompt_kernel(q_ref, kp_ref, kc_ref, vp_ref, vc_ref, sink_ref, o_ref, qm_scr):
    i = pl.program_id(2)
    w = WINDOW
    _masked_q(q_ref, qm_scr)
    kcat = jnp.concatenate([kp_ref[0], kc_ref[0]], axis=0)
    vcat = jnp.concatenate([vp_ref[0], vc_ref[0]], axis=0)
    r = _iota((w, 2 * w), 0)
    col = _iota((w, 2 * w), 1)
    lim = jnp.where(i > 0, r, 2 * w)
    mask = ((col > lim) & (col < w)) | ((col >= w) & (col - w <= r))
    heads = range(HEADS_PER_STEP)
    scores = [jnp.where(mask, lax.dot_general(qm_scr[h], kcat, NT_DIMS, preferred_element_type=F32), NEG_INF)
              for h in heads]
    probs, inv_l = [], []
    for h in heads:
        sink = sink_ref[0, h:h + 1, :] * LOG2E
        m = jnp.maximum(jnp.max(scores[h], axis=-1, keepdims=True), sink)
        p = jnp.exp2(scores[h] - jnp.tile(m, (1, 2 * w // LANES)))
        inv_l.append(1.0 / (jnp.sum(p, axis=-1, keepdims=True) + jnp.exp2(sink - m)))
        probs.append(p.astype(BF16))
    acc = [jnp.dot(probs[h], vcat, preferred_element_type=F32) for h in heads]
    _write_heads(o_ref, acc, inv_l)


def swa_prompt(q, kb, vb, sink_perm):
    b, t, _ = q.shape
    w = WINDOW
    nb = t // w
    npair = N_KV // 2
    sink_arr = jnp.broadcast_to(sink_perm.astype(F32).reshape(npair, HEADS_PER_STEP, 1), (npair, HEADS_PER_STEP, LANES))
    qspec = pl.BlockSpec((1, w, SLABS_PER_STEP * LANES), lambda bi, j, i: (bi, i, j))
    cur = pl.BlockSpec((1, w, LANES), lambda bi, j, i: (bi, i, j))
    prev = pl.BlockSpec((1, w, LANES), lambda bi, j, i: (bi, jnp.maximum(i - 1, 0), j))
    return pl.pallas_call(
        _swa_prompt_kernel,
        grid=(b, npair, nb),
        in_specs=[qspec, prev, cur, prev, cur, pl.BlockSpec((1, HEADS_PER_STEP, LANES), lambda bi, j, i: (j, 0, 0))],
        out_specs=qspec,
        out_shape=jax.ShapeDtypeStruct(q.shape, BF16),
        scratch_shapes=[pltpu.VMEM((HEADS_PER_STEP, w, LANES), BF16)],
        compiler_params=_cparams(("parallel", "parallel", "arbitrary")),
        name="swa_prompt",
    )(q, kb, kb, vb, vb, sink_arr)


def _cumsum_t_kernel(lf_ref, o_ref, carry_scr, *, rows):
    c = pl.program_id(1)

    @pl.when(c == 0)
    def _():
        carry_scr[...] = jnp.zeros(carry_scr.shape, F32)

    L = lf_ref.shape[1]
    tril_b = (_iota((L, L), 0) >= _iota((L, L), 1)).astype(BF16)
    cs = _dot01_left(tril_b, lf_ref[0]) + carry_scr[...]
    carry_scr[...] = cs[L - 1:L, :]
    o_ref[0] = cs.T[0:rows, :] * LOG2E


def cumsum_t(lf, rows):
    b, t, _ = lf.shape
    L = 128
    return pl.pallas_call(
        functools.partial(_cumsum_t_kernel, rows=rows),
        grid=(b, t // L),
        in_specs=[pl.BlockSpec((1, L, LANES), lambda i, j: (i, j, 0))],
        out_specs=pl.BlockSpec((1, rows, L), lambda i, j: (i, 0, j)),
        out_shape=jax.ShapeDtypeStruct((b, rows, t), F32),
        scratch_shapes=[pltpu.VMEM((1, LANES), F32)],
        compiler_params=_cparams(("parallel", "arbitrary")),
        name="cumsum_t",
    )(lf)


def _fox_prompt_kernel(qi_ref, ki_ref, q_ref, k_ref, v_ref, c_ref, o_ref, qm_scr, m_scr, l_scr, acc_scr, *, tq, tk):
    step = pl.program_id(2)
    qi = qi_ref[step]
    ki = ki_ref[step]
    last_k = (qi * tq + tq - 1) // tk

    @pl.when(ki == 0)
    def _():
        _masked_q(q_ref, qm_scr)
        m_scr[...] = jnp.full(m_scr.shape, NEG_INF, F32)
        l_scr[...] = jnp.zeros(l_scr.shape, F32)
        acc_scr[...] = jnp.zeros(acc_scr.shape, F32)

    def body(masked):
        kb = k_ref[0]
        vb = v_ref[0]
        heads = range(HEADS_PER_STEP)
        if masked:
            row = qi * tq + _iota((tq, tk), 0)
            col = ki * tk + _iota((tq, tk), 1)
            keep = col <= row
        scores = []
        for h in heads:
            s = lax.dot_general(qm_scr[h], kb, NT_DIMS, preferred_element_type=F32) - c_ref[0, h:h + 1, :]
            scores.append(jnp.where(keep, s, NEG_INF) if masked else s)
        probs, alphas = [], []
        for h in heads:
            m_prev = m_scr[h]
            m_new = jnp.maximum(m_prev, jnp.max(scores[h], axis=-1, keepdims=True))
            alpha = jnp.exp2(m_prev - m_new)
            p = jnp.exp2(scores[h] - jnp.tile(m_new, (1, tk // LANES)))
            l_scr[h] = alpha * l_scr[h] + jnp.sum(p, axis=-1, keepdims=True)
            m_scr[h] = m_new
            probs.append(p.astype(BF16))
            alphas.append(alpha)
        for h in heads:
            acc_scr[h] = alphas[h] * acc_scr[h] + jnp.dot(probs[h], vb, preferred_element_type=F32)

    needs_mask = (ki + 1) * tk - 1 > qi * tq

    @pl.when(needs_mask)
    def _():
        body(True)

    @pl.when(jnp.logical_not(needs_mask))
    def _():
        body(False)

    @pl.when(ki == last_k)
    def _():
        _write_heads(o_ref, [acc_scr[h] for h in range(HEADS_PER_STEP)],
                     [1.0 / l_scr[h] for h in range(HEADS_PER_STEP)])


def fox_prompt(q, kb, vb, cum_t, tq=512, tk=512):
    b, t, _ = q.shape
    tq, tk = min(tq, t), min(tk, t)
    npair = N_KV // 2
    qi_l, ki_l = [], []
    for qi in range(t // tq):
        for ki in range((qi * tq + tq - 1) // tk + 1):
            qi_l.append(qi)
            ki_l.append(ki)
    qi_arr = jnp.asarray(qi_l, jnp.int32)
    ki_arr = jnp.asarray(ki_l, jnp.int32)
    qspec = pl.BlockSpec((1, tq, SLABS_PER_STEP * LANES), lambda bi, j, s, qi, ki: (bi, qi[s], j))
    kspec = pl.BlockSpec((1, tk, LANES), lambda bi, j, s, qi, ki: (bi, ki[s], j))
    cspec = pl.BlockSpec((1, HEADS_PER_STEP, tk), lambda bi, j, s, qi, ki: (bi, j, ki[s]))
    grid_spec = pltpu.PrefetchScalarGridSpec(
        num_scalar_prefetch=2,
        grid=(b, npair, len(qi_l)),
        in_specs=[qspec, kspec, kspec, cspec],
        out_specs=qspec,
        scratch_shapes=[pltpu.VMEM((HEADS_PER_STEP, tq, LANES), BF16),
                        pltpu.VMEM((HEADS_PER_STEP, tq, LANES), F32),
                        pltpu.VMEM((HEADS_PER_STEP, tq, LANES), F32),
                        pltpu.VMEM((HEADS_PER_STEP, tq, LANES), F32)],
    )
    return pl.pallas_call(
        functools.partial(_fox_prompt_kernel, tq=tq, tk=tk),
        grid_spec=grid_spec,
        out_shape=jax.ShapeDtypeStruct(q.shape, BF16),
        compiler_params=_cparams(("parallel", "parallel", "arbitrary")),
        name="fox_prompt",
    )(qi_arr, ki_arr, q, kb, vb, cum_t)


def _xattn_kernel(q_ref, k_ref, v_ref, o_ref, *, nb, mem, interleaved):
    scale = XA_DIM ** -0.5
    for bi in range(nb):
        for h in range(XA_HEADS):
            sl = slice(h * XA_DIM, (h + 1) * XA_DIM)
            qh = q_ref[bi, :, sl]
            if interleaved:
                rows = pl.ds(h, mem, stride=XA_HEADS)
                kh = k_ref[bi, rows, :].astype(BF16)
                vh = v_ref[bi, rows, :].astype(BF16)
            else:
                kh = k_ref[bi, :, sl].astype(BF16)
                vh = v_ref[bi, :, sl].astype(BF16)
            s = lax.dot_general(qh, kh, NT_DIMS, preferred_element_type=F32) * scale
            m = jnp.max(s, axis=-1, keepdims=True)
            p = jnp.exp(s - m)
            l = jnp.sum(p, axis=-1, keepdims=True)
            o = jnp.dot(p.astype(BF16), vh, preferred_element_type=F32) * (1.0 / l)
            o_ref[bi, :, sl] = o.astype(o_ref.dtype)


def xattn(q, mk, mv, tq, nb):
    b, t, _ = q.shape
    interleaved = mk.shape[2] == XA_DIM
    mem = mk.shape[1] // XA_HEADS if interleaved else mk.shape[1]
    qspec = pl.BlockSpec((nb, tq, XA_WIDTH), lambda i, j: (i, j, 0))
    mspec = pl.BlockSpec((nb,) + mk.shape[1:], lambda i, j: (i, 0, 0))
    return pl.pallas_call(
        functools.partial(_xattn_kernel, nb=nb, mem=mem, interleaved=interleaved),
        grid=(b // nb, t // tq),
        in_specs=[qspec, mspec, mspec],
        out_specs=qspec,
        out_shape=jax.ShapeDtypeStruct(q.shape, BF16),
        compiler_params=_cparams(("parallel", "arbitrary")),
        name="xattn",
    )(q, mk, mv)


def _block_diag_q(q, nb, t):
    q4 = q.reshape(nb, t * N_HEADS, HEAD_DIM)
    qt = jnp.tile(q4, (1, 1, N_KV))
    row_kv = (np.arange(t * N_HEADS) % N_HEADS) // GROUP
    keep = jnp.asarray(row_kv[:, None] == (np.arange(N_KV * HEAD_DIM) // HEAD_DIM)[None, :])
    return jnp.where(keep[None], qt * jnp.asarray(HEAD_DIM ** -0.5, BF16), jnp.zeros((), BF16))


def _diag_extract(acc, rows):
    row_kv = (_iota((rows, HEAD_DIM), 0) % N_HEADS) // GROUP
    out = jnp.zeros((rows, HEAD_DIM), F32)
    for kv in range(N_KV):
        out = out + jnp.where(row_kv == kv, acc[:, kv * HEAD_DIM:(kv + 1) * HEAD_DIM], 0.0)
    return out


NEW_PAD = 16


def _swa_sample_kernel(q_ref, ck_ref, cv_ref, kn_ref, vn_ref, sink_ref, o_ref, ok_ref, ov_ref, *, nb, t):
    w = WINDOW
    rows = t * N_HEADS
    kvw = N_KV * HEAD_DIM
    tok = _iota((rows, 2 * w), 0) // N_HEADS
    col = _iota((rows, 2 * w), 1)
    mask = ((col > tok) & (col < w)) | ((col >= w) & (col - w <= tok))
    sink = sink_ref[:, 0:1]
    zpad = jnp.zeros((w - NEW_PAD, kvw), BF16)
    for bi in range(nb):
        ck = ck_ref[bi]
        cv = cv_ref[bi]
        kn = kn_ref[bi]
        vn = vn_ref[bi]
        kcat = jnp.concatenate([ck.astype(BF16), kn.astype(BF16), zpad], axis=0)
        vcat = jnp.concatenate([cv.astype(BF16), vn.astype(BF16), zpad], axis=0)
        s = lax.dot_general(q_ref[bi], kcat, NT_DIMS, preferred_element_type=F32)
        s = jnp.where(mask, s, NEG_INF)
        m = jnp.maximum(jnp.max(s, axis=-1, keepdims=True), sink)
        p = jnp.exp(s - m)
        l = jnp.sum(p, axis=-1, keepdims=True) + jnp.exp(sink - m)
        acc = jnp.dot(p.astype(BF16), vcat, preferred_element_type=F32)
        o_ref[bi] = (_diag_extract(acc, rows) * (1.0 / l)).astype(o_ref.dtype)
        ok_ref[bi, 0:w - t, :] = ck[t:w, :]
        ov_ref[bi, 0:w - t, :] = cv[t:w, :]
        ok_ref[bi, w - t:w, :] = kn[0:t, :]
        ov_ref[bi, w - t:w, :] = vn[0:t, :]


def swa_sample(qbd, ck, cv, kn, vn, sink, t, nb=8):
    b = qbd.shape[0]
    w = WINDOW
    rows = t * N_HEADS
    sink_rows = jnp.broadcast_to(jnp.tile(sink.astype(F32), t).reshape(rows, 1), (rows, LANES))
    bspec = lambda s1, s2: pl.BlockSpec((nb, s1, s2), lambda i: (i, 0, 0))
    kvw = N_KV * HEAD_DIM
    return pl.pallas_call(
        functools.partial(_swa_sample_kernel, nb=nb, t=t),
        grid=(b // nb,),
        in_specs=[bspec(rows, kvw), bspec(w, kvw), bspec(w, kvw), bspec(NEW_PAD, kvw), bspec(NEW_PAD, kvw),
                  _const_spec((rows, LANES))],
        out_specs=[bspec(rows, HEAD_DIM), bspec(w, kvw), bspec(w, kvw)],
        out_shape=[jax.ShapeDtypeStruct((b, rows, HEAD_DIM), BF16),
                   jax.ShapeDtypeStruct((b, w, kvw), F32), jax.ShapeDtypeStruct((b, w, kvw), F32)],
        compiler_params=_cparams(("parallel",)),
        name="swa_sample",
    )(qbd, ck, cv, kn, vn, sink_rows)


FOX_PAGES_PER_STEP = 16


def _fox_sample_kernel(pt_ref, q_ref, kn_ref, vn_ref, lfn_ref, kc_hbm, vc_hbm, lf_hbm, o_ref,
                       kbuf, vbuf, lbuf, sem, carry_scr, m_scr, l_scr, acc_scr,
                       *, t, n_pages, pg):
    b = pl.program_id(0)
    c = pl.program_id(1)
    nch = pl.num_programs(1)
    step = b * nch + c
    total = pl.num_programs(0) * nch
    rows = t * N_HEADS
    nh = N_HEADS

    def copies(stp, slot):
        base = (stp // nch) * n_pages + (stp % nch) * pg
        out = []
        for p in range(pg):
            page = pt_ref[base + p]
            out.append(pltpu.make_async_copy(kc_hbm.at[page], kbuf.at[slot, :, pl.ds(p * PAGE, PAGE)], sem.at[slot, 0]))
            out.append(pltpu.make_async_copy(vc_hbm.at[page], vbuf.at[slot, :, pl.ds(p * PAGE, PAGE)], sem.at[slot, 1]))
            out.append(pltpu.make_async_copy(lf_hbm.at[page], lbuf.at[slot, pl.ds(p * nh, nh)], sem.at[slot, 2]))
        return out

    slot = step % 2

    @pl.when(step == 0)
    def _():
        for cp in copies(step, slot):
            cp.start()

    @pl.when(step + 1 < total)
    def _():
        for cp in copies(step + 1, 1 - slot):
            cp.start()

    @pl.when(c == 0)
    def _():
        carry_scr[...] = jnp.zeros(carry_scr.shape, F32)
        m_scr[...] = jnp.full(m_scr.shape, NEG_INF, F32)
        l_scr[...] = jnp.zeros(l_scr.shape, F32)
        acc_scr[...] = jnp.zeros(acc_scr.shape, F32)

    for cp in copies(step, slot):
        cp.wait()

    q = q_ref[0]
    upper_b = (_iota((PAGE, PAGE), 0) <= _iota((PAGE, PAGE), 1)).astype(BF16)

    def online(s, pv):
        m_prev = m_scr[...]
        m_new = jnp.maximum(m_prev, jnp.max(s, axis=-1, keepdims=True))
        alpha = jnp.exp(m_prev - m_new)
        p = jnp.exp(s - m_new)
        l_scr[...] = alpha * l_scr[...] + jnp.sum(p, axis=-1, keepdims=True)
        acc_scr[...] = alpha * acc_scr[...] + pv(p.astype(BF16))
        m_scr[...] = m_new

    lf = lbuf[slot]
    cl = _dot01_right(lf, upper_b)
    tot = jnp.broadcast_to(cl[:, PAGE - 1:PAGE], cl.shape)
    r = _iota((pg * nh, pg * nh), 0)
    cc = _iota((pg * nh, pg * nh), 1)
    earlier_b = ((r % nh == cc % nh) & (cc // nh < r // nh)).astype(BF16)
    cum = cl + _dot01_left(earlier_b, tot) + jnp.concatenate([carry_scr[...]] * pg, axis=0)
    carry_new = jnp.broadcast_to(cum[(pg - 1) * nh:pg * nh, PAGE - 1:PAGE], (nh, LANES))
    carry_scr[...] = carry_new
    bias16 = jnp.concatenate([cum[p * nh:(p + 1) * nh, :] for p in range(pg)], axis=1)
    bias = jnp.concatenate([bias16] * t, axis=0)

    kc = kbuf[slot].astype(BF16)
    vc = vbuf[slot].astype(BF16)
    s = jnp.dot(q, kc, preferred_element_type=F32) - bias
    online(s, lambda p: lax.dot_general(p, vc, NT_DIMS, preferred_element_type=F32))

    @pl.when(c == nch - 1)
    def _():
        zpad = jnp.zeros((PAGE - NEW_PAD, N_KV * HEAD_DIM), BF16)
        knew = jnp.concatenate([kn_ref[0].astype(BF16), zpad], axis=0)
        vnew = jnp.concatenate([vn_ref[0].astype(BF16), zpad], axis=0)
        cum_n = _dot01_right(lfn_ref[0], upper_b) + carry_new
        sn = lax.dot_general(q, knew, NT_DIMS, preferred_element_type=F32)
        sn = sn - jnp.concatenate([cum_n] * t, axis=0)
        tok = _iota((rows, PAGE), 0) // nh
        sn = jnp.where(_iota((rows, PAGE), 1) <= tok, sn, NEG_INF)
        online(sn, lambda p: jnp.dot(p, vnew, preferred_element_type=F32))
        o_ref[0] = (_diag_extract(acc_scr[...], rows) * (1.0 / l_scr[...])).astype(o_ref.dtype)


def fox_sample(page_table, qbd, kn, vn, lfn_t, kcache, vcache, lf_t_cache, t, pg=FOX_PAGES_PER_STEP):
    b, n_pages = page_table.shape
    pg = min(pg, n_pages)
    assert n_pages % pg == 0
    rows = t * N_HEADS
    kvw = N_KV * HEAD_DIM
    bspec = lambda s1, s2: pl.BlockSpec((1, s1, s2), lambda i, j, pt: (i, 0, 0))
    any_spec = pl.BlockSpec(memory_space=pl.ANY)
    grid_spec = pltpu.PrefetchScalarGridSpec(
        num_scalar_prefetch=1,
        grid=(b, n_pages // pg),
        in_specs=[bspec(rows, kvw), bspec(NEW_PAD, kvw), bspec(NEW_PAD, kvw), bspec(N_HEADS, LANES),
                  any_spec, any_spec, any_spec],
        out_specs=bspec(rows, HEAD_DIM),
        scratch_shapes=[pltpu.VMEM((2, kvw, pg * PAGE), F32), pltpu.VMEM((2, kvw, pg * PAGE), F32),
                        pltpu.VMEM((2, pg * N_HEADS, LANES), F32), pltpu.SemaphoreType.DMA((2, 3)),
                        pltpu.VMEM((N_HEADS, LANES), F32),
                        pltpu.VMEM((rows, 1), F32), pltpu.VMEM((rows, 1), F32), pltpu.VMEM((rows, kvw), F32)],
    )
    return pl.pallas_call(
        functools.partial(_fox_sample_kernel, t=t, n_pages=n_pages, pg=pg),
        grid_spec=grid_spec,
        out_shape=jax.ShapeDtypeStruct((b, rows, HEAD_DIM), BF16),
        compiler_params=_cparams(("arbitrary", "arbitrary")),
        name="fox_sample",
    )(page_table.reshape(-1), qbd, kn, vn, lfn_t, kcache, vcache, lf_t_cache)


def _pad_cols(w, width):
    return jnp.pad(w, ((0, 0), (0, width - w.shape[1])))


def kernel(x_prompt, x_sample, state_ssm, state_conv, cache_swa_k, cache_swa_v, cache_fox_k, cache_fox_v,
           cache_fox_logf, cache_mem_k, cache_mem_v, page_table, mem_prompt, norm_mix, norm_xa, norm_mem,
           norm_ffn, w_in_even, conv_w, conv_b, dt_bias, a_log, d_skip, ssm_norm, swa_sink, w_out_even,
           w_in_odd, fox_fb, w_out_odd, w_xq, w_xk, w_xv, w_xo, w_ffn_in, w_ffn_out, norm_final):
    bp, seq, d = x_prompt.shape
    bs, ts, _ = x_sample.shape
    depth = norm_mix.shape[0]
    mem_len = mem_prompt.shape[1]
    kvw = N_KV * HEAD_DIM
    qw = N_HEADS * HEAD_DIM
    xp = x_prompt.reshape(bp * seq, d)
    xs = x_sample.reshape(bs * ts, d)
    mem2 = mem_prompt.reshape(bp * mem_len, d)
    outs = {k: [] for k in ("p_ssm", "p_conv", "p_swk", "p_swv", "p_fk", "p_fv", "p_fl", "p_mk", "p_mv",
                            "s_ssm", "s_conv", "s_swk", "s_swv", "s_fk", "s_fv", "s_fl")}

    for l in range(depth):
        li = l // 2
        if l % 2 == 0:
            w = w_in_even[li]
            o_z, o_xbc, o_dt = 0, SSM_WIDTH, SSM_WIDTH + CONV_DIM
            o_q = o_dt + SSM_HEADS
            o_k, o_v = o_q + qw, o_q + qw + kvw
            wz, wxbc, wdt = w[:, o_z:o_xbc], w[:, o_xbc:o_dt], _pad_cols(w[:, o_dt:o_q], LANES)
            wq, wk, wv = w[:, o_q:o_k], w[:, o_k:o_v], w[:, o_v:o_v + kvw]
            c_xbc, c_q, c_k, c_v, c_dt = SSM_WIDTH, SSM_WIDTH + CONV_DIM, SSM_WIDTH + CONV_DIM + qw, \
                SSM_WIDTH + CONV_DIM + qw + kvw, SSM_WIDTH + CONV_DIM + qw + 2 * kvw
            n_tot = c_dt + LANES
            wo = w_out_even[li]
            wo_ssm = wo[:SSM_WIDTH].astype(BF16)
            wo_att = wo[SSM_WIDTH:]
            ssm_params = (conv_w[li], conv_b[li], dt_bias[li], a_log[li], d_skip[li], ssm_norm[li])

            w_p = jnp.concatenate([wz, wxbc, wq[:, COL_PERM], wk, wv, wdt], axis=1).astype(BF16)
            segs_p = [(0, c_xbc, (F32,), None), (c_xbc, c_q, (F32,), None), (c_q, c_k, (BF16,), Q_SCALE),
                      (c_k, c_v, (F32, BF16), None), (c_v, c_dt, (F32, BF16), None), (c_dt, n_tot, (F32,), None)]
            z, xbc, q, k, kb, v, vb, dtr = norm_proj(xp, norm_mix[l], w_p, segs_p)
            xbc3 = xbc.reshape(bp, seq, CONV_DIM)
            y_ssm, h = ssd(xbc3, z.reshape(bp, seq, SSM_WIDTH), dtr.reshape(bp, seq, LANES),
                           jnp.zeros((bp, 8, CONV_DIM), F32), None, *ssm_params, L=SSM_CHUNK, valid=SSM_CHUNK)
            y_att = swa_prompt(q.reshape(bp, seq, qw), kb.reshape(bp, seq, kvw), vb.reshape(bp, seq, kvw),
                               swa_sink[li][HEAD_PERM])
            xp = matmul_residual([y_ssm.reshape(bp * seq, SSM_WIDTH), y_att.reshape(bp * seq, qw)],
                                 [wo_ssm, wo_att[COL_PERM].astype(BF16)], xp)
            outs["p_ssm"].append(h.reshape(bp, SSM_HEADS, SSM_P, SSM_N))
            outs["p_conv"].append(xbc3[:, seq - (CONV_K - 1):])
            outs["p_swk"].append(k.reshape(bp, seq, N_KV, HEAD_DIM)[:, seq - WINDOW:])
            outs["p_swv"].append(v.reshape(bp, seq, N_KV, HEAD_DIM)[:, seq - WINDOW:])

            w_s = jnp.concatenate([wz, wxbc, wq, wk, wv, wdt], axis=1).astype(BF16)
            segs_s = [(0, c_xbc, (F32,), None), (c_xbc, c_q, (F32,), None), (c_q, c_k, (BF16,), None),
                      (c_k, c_v, (F32,), None), (c_v, c_dt, (F32,), None), (c_dt, n_tot, (F32,), None)]
            z, xbc, q, k, v, dtr = norm_proj(xs, norm_mix[l], w_s, segs_s)
            ls = NEW_PAD
            pad_t = lambda a: jnp.pad(a.reshape(bs, ts, a.shape[-1]), ((0, 0), (0, ls - ts), (0, 0)))
            xbc3 = xbc.reshape(bs, ts, CONV_DIM)
            tail = jnp.pad(state_conv[li].astype(F32), ((0, 0), (8 - (CONV_K - 1), 0), (0, 0)))
            y_ssm, h = ssd(pad_t(xbc), pad_t(z), pad_t(dtr), tail, state_ssm[li].reshape(bs, SSM_WIDTH, SSM_N),
                           *ssm_params, L=ls, valid=ts)
            y_ssm = y_ssm[:, :ts].reshape(bs * ts, SSM_WIDTH)
            kn, vn = pad_t(k), pad_t(v)
            o_att, new_k, new_v = swa_sample(_block_diag_q(q, bs, ts), cache_swa_k[li].reshape(bs, WINDOW, kvw),
                                             cache_swa_v[li].reshape(bs, WINDOW, kvw), kn, vn, swa_sink[li], ts)
            xs = matmul_residual([y_ssm, o_att.reshape(bs * ts, qw)], [wo_ssm, wo_att.astype(BF16)], xs)
            outs["s_ssm"].append(h.reshape(bs, SSM_HEADS, SSM_P, SSM_N))
            outs["s_conv"].append(jnp.concatenate([state_conv[li].astype(F32), xbc3], axis=1)[:, -(CONV_K - 1):])
            outs["s_swk"].append(new_k.reshape(bs, WINDOW, N_KV, HEAD_DIM))
            outs["s_swv"].append(new_v.reshape(bs, WINDOW, N_KV, HEAD_DIM))
        else:
            w = w_in_odd[li]
            wq, wk, wv, wf = w[:, :qw], w[:, qw:qw + kvw], w[:, qw + kvw:qw + 2 * kvw], w[:, qw + 2 * kvw:]
            c_k, c_v, c_f = qw, qw + kvw, qw + 2 * kvw
            n_tot = c_f + LANES
            fb = fox_fb[li].astype(F32)

            wf_p = _pad_cols(jnp.concatenate([wf[:, HEAD_PERM], wf], axis=1), LANES)
            fb_p = jnp.pad(jnp.concatenate([fb[HEAD_PERM], fb]), (0, LANES - 2 * N_HEADS)).reshape(1, LANES)
            w_p = jnp.concatenate([wq[:, COL_PERM], wk, wv, wf_p], axis=1).astype(BF16)
            segs_p = [(0, c_k, (BF16,), Q_SCALE), (c_k, c_v, (F32, BF16), None), (c_v, c_f, (F32, BF16), None),
                      (c_f, n_tot, (F32,), "logsig")]
            q, k, kb, v, vb, lf = norm_proj(xp, norm_mix[l], w_p, segs_p, aux=fb_p)
            lf3 = lf.reshape(bp, seq, LANES)
            cum_t = cumsum_t(lf3, N_HEADS)
            y = fox_prompt(q.reshape(bp, seq, qw), kb.reshape(bp, seq, kvw), vb.reshape(bp, seq, kvw), cum_t)
            xp = matmul_residual([y.reshape(bp * seq, qw)], [w_out_odd[li][COL_PERM].astype(BF16)], xp)
            outs["p_fk"].append(k.reshape(bp, seq, N_KV, HEAD_DIM))
            outs["p_fv"].append(v.reshape(bp, seq, N_KV, HEAD_DIM))
            outs["p_fl"].append(lf3[:, :, N_HEADS:2 * N_HEADS])

            w_s = jnp.concatenate([wq, wk, wv, _pad_cols(wf, LANES)], axis=1).astype(BF16)
            fb_s = jnp.pad(fb, (0, LANES - N_HEADS)).reshape(1, LANES)
            segs_s = [(0, c_k, (BF16,), None), (c_k, c_v, (F32,), None), (c_v, c_f, (F32,), None),
                      (c_f, n_tot, (F32,), "logsig")]
            q, k, v, lf = norm_proj(xs, norm_mix[l], w_s, segs_s, aux=fb_s)
            lf_new = lf[:, :N_HEADS].reshape(bs, ts, N_HEADS)
            pad_t = lambda a: jnp.pad(a.reshape(bs, ts, a.shape[-1]), ((0, 0), (0, NEW_PAD - ts), (0, 0)))
            lfn_t = jnp.pad(jnp.swapaxes(lf_new, 1, 2), ((0, 0), (0, 0), (0, LANES - ts)))
            pool = cache_fox_k.shape[1]
            page_t = lambda c: jnp.transpose(c[li].astype(F32), (0, 2, 3, 1)).reshape(pool, kvw, PAGE)
            y = fox_sample(page_table, _block_diag_q(q, bs, ts), pad_t(k), pad_t(v), lfn_t,
                           page_t(cache_fox_k), page_t(cache_fox_v),
                           jnp.swapaxes(cache_fox_logf[li].astype(F32), 1, 2), ts)
            xs = matmul_residual([y.reshape(bs * ts, qw)], [w_out_odd[li].astype(BF16)], xs)
            outs["s_fk"].append(k.reshape(bs, ts, N_KV, HEAD_DIM))
            outs["s_fv"].append(v.reshape(bs, ts, N_KV, HEAD_DIM))
            outs["s_fl"].append(lf_new)

        wkv = jnp.concatenate([w_xk[l], w_xv[l]], axis=1).astype(BF16)
        mk, mv = norm_proj(mem2, norm_mem[l], wkv, [(0, XA_WIDTH, (F32,), None), (XA_WIDTH, 2 * XA_WIDTH, (F32,), None)])
        wq_b = w_xq[l].astype(BF16)
        wo_b = w_xo[l].astype(BF16)
        (qx,) = norm_proj(xp, norm_xa[l], wq_b, [(0, XA_WIDTH, (BF16,), None)])
        ox = xattn(qx.reshape(bp, seq, XA_WIDTH), mk.reshape(bp, mem_len, XA_WIDTH), mv.reshape(bp, mem_len, XA_WIDTH),
                   tq=512, nb=1)
        xp = matmul_residual([ox.reshape(bp * seq, XA_WIDTH)], [wo_b], xp)
        (qx,) = norm_proj(xs, norm_xa[l], wq_b, [(0, XA_WIDTH, (BF16,), None)])
        qx = jnp.pad(qx.reshape(bs, ts, XA_WIDTH), ((0, 0), (0, NEW_PAD - ts), (0, 0)))
        ox = xattn(qx, cache_mem_k[l].reshape(bs, mem_len * XA_HEADS, XA_DIM),
                   cache_mem_v[l].reshape(bs, mem_len * XA_HEADS, XA_DIM), tq=NEW_PAD, nb=4)
        xs = matmul_residual([ox[:, :ts].reshape(bs * ts, XA_WIDTH)], [wo_b], xs)
        outs["p_mk"].append(mk.reshape(bp, mem_len, XA_HEADS, XA_DIM))
        outs["p_mv"].append(mv.reshape(bp, mem_len, XA_HEADS, XA_DIM))

        wg = w_ffn_in[l][:, :FFN_H].astype(BF16)
        wu = w_ffn_in[l][:, FFN_H:].astype(BF16)
        wd = w_ffn_out[l].astype(BF16)
        gf = norm_final if l == depth - 1 else None
        xp = ffn(xp, norm_ffn[l], wg, wu, wd, g_final=gf)
        xs = ffn(xs, norm_ffn[l], wg, wu, wd, g_final=gf)

    st = lambda key: jnp.stack(outs[key])
    return (xp.reshape(bp, seq, d), xs.reshape(bs, ts, d),
            st("p_ssm"), st("p_conv"), st("p_swk"), st("p_swv"), st("p_fk"), st("p_fv"), st("p_fl"),
            st("p_mk"), st("p_mv"), st("s_ssm"), st("s_conv"), st("s_swk"), st("s_swv"),
            st("s_fk"), st("s_fv"), st("s_fl"))
```

```python
import functools
import math

import numpy as np
import jax
import jax.numpy as jnp
from jax import lax
from jax.experimental import pallas as pl
from jax.experimental.pallas import tpu as pltpu

F32 = jnp.float32
BF16 = jnp.bfloat16

D_MODEL = 1024
HEAD_DIM = 64
N_HEADS = 16
N_KV = 4
SSM_HEADS = 16
SSM_P = 64
SSM_WIDTH = 1024
SSM_N = 128
SSM_GROUPS = 2
SSM_CHUNK = 128
CONV_K = 4
CONV_DIM = 1536
WINDOW = 128
PAGE = 128
XA_HEADS = 4
XA_DIM = 128
XA_WIDTH = 512
FFN_H = 2816
RMS_EPS = 1e-6
NEG_INF = -1e30

V7X_VMEM_BYTES = 64 * 1024 * 1024
VMEM_LIMIT = V7X_VMEM_BYTES - 8 * 1024 * 1024
LANES = 128

NT_DIMS = (((1,), (1,)), ((), ()))


def _cparams(sem):
    return pltpu.CompilerParams(dimension_semantics=sem, vmem_limit_bytes=VMEM_LIMIT)


def _const_spec(shape):
    nd = len(shape)
    return pl.BlockSpec(shape, lambda *_: (0,) * nd, pipeline_mode=pl.Buffered(1))


def _softplus(x):
    return jnp.maximum(x, 0.0) + jnp.log1p(jnp.exp(-jnp.abs(x)))


def _silu(x):
    return x * (1.0 / (1.0 + jnp.exp(-x)))


def _split3(x):
    hi = x.astype(BF16)
    r1 = x - hi.astype(F32)
    mid = r1.astype(BF16)
    lo = (r1 - mid.astype(F32)).astype(BF16)
    return hi, mid, lo


def _dot01_left(m01, x):
    hi, mid, lo = _split3(x)
    d = lambda b: jnp.dot(m01, b, preferred_element_type=F32)
    return (d(lo) + d(mid)) + d(hi)


def _dot01_right(x, m01):
    hi, mid, lo = _split3(x)
    d = lambda a: jnp.dot(a, m01, preferred_element_type=F32)
    return (d(lo) + d(mid)) + d(hi)


def _iota(shape, dim):
    return lax.broadcasted_iota(jnp.int32, shape, dim)


def _norm_proj_kernel(*refs, segs, has_aux):
    x_ref, g_ref, w_ref = refs[:3]
    pos = 3
    aux_ref = None
    if has_aux:
        aux_ref = refs[3]
        pos = 4
    out_refs = refs[pos:]
    x = x_ref[...]
    ms = jnp.mean(x * x, axis=-1, keepdims=True)
    xn = (x * lax.rsqrt(ms + RMS_EPS)) * g_ref[...]
    xb = xn.astype(BF16)
    oi = 0
    for (c0, c1, dtypes, act) in segs:
        y = jnp.dot(xb, w_ref[:, c0:c1], preferred_element_type=F32)
        if act == "logsig":
            y = -_softplus(-(y + aux_ref[...]))
        elif act is not None:
            y = y * act
        for dt in dtypes:
            if dt == "T":
                out_refs[oi][0] = y.T
            else:
                out_refs[oi][...] = y.astype(dt)
            oi += 1


def norm_proj(x, g, w, segs, aux=None, tm=512, rows_per_batch=None):
    m, d = x.shape
    tm = min(tm, m)
    assert m % tm == 0
    n = w.shape[1]
    tiles_per_b = (rows_per_batch or m) // tm
    in_specs = [pl.BlockSpec((tm, d), lambda i: (i, 0)), _const_spec((1, d)), _const_spec((d, n))]
    args = [x, g.reshape(1, d).astype(F32), w]
    if aux is not None:
        in_specs.append(_const_spec(aux.shape))
        args.append(aux)
    out_shape, out_specs = [], []
    for (c0, c1, dtypes, _) in segs:
        for dt in dtypes:
            if dt == "T":
                out_shape.append(jax.ShapeDtypeStruct((m // rows_per_batch, c1 - c0, rows_per_batch), F32))
                out_specs.append(pl.BlockSpec((1, c1 - c0, tm), lambda i: (i // tiles_per_b, 0, i % tiles_per_b)))
            else:
                out_shape.append(jax.ShapeDtypeStruct((m, c1 - c0), dt))
                out_specs.append(pl.BlockSpec((tm, c1 - c0), lambda i: (i, 0)))
    return pl.pallas_call(
        functools.partial(_norm_proj_kernel, segs=tuple(segs), has_aux=aux is not None),
        grid=(m // tm,),
        in_specs=in_specs,
        out_specs=out_specs,
        out_shape=out_shape,
        compiler_params=_cparams(("parallel",)),
        name="norm_proj",
    )(*args)


def _mm_res_kernel(*refs, n, proj):
    a_refs, w_refs = refs[:n], refs[n:2 * n]
    res_ref = refs[2 * n]
    acc = res_ref[...]
    for a_ref, w_ref in zip(a_refs, w_refs):
        acc = acc + jnp.dot(a_ref[...].astype(BF16), w_ref[...], preferred_element_type=F32)
    if proj:
        g_ref, wp_ref, o_ref, p_ref = refs[2 * n + 1:]
        ms = jnp.mean(acc * acc, axis=-1, keepdims=True)
        xb = ((acc * lax.rsqrt(ms + RMS_EPS)) * g_ref[...]).astype(BF16)
        p_ref[...] = jnp.dot(xb, wp_ref[...], preferred_element_type=F32).astype(p_ref.dtype)
    else:
        o_ref = refs[2 * n + 1]
    o_ref[...] = acc


def matmul_residual(a_list, w_list, res, proj=None, tm=512):
    m, d = res.shape
    tm = min(tm, m)
    assert m % tm == 0
    n = len(a_list)
    in_specs = [pl.BlockSpec((tm, a.shape[1]), lambda i: (i, 0)) for a in a_list]
    in_specs += [_const_spec(w.shape) for w in w_list]
    in_specs.append(pl.BlockSpec((tm, d), lambda i: (i, 0)))
    args = [*a_list, *w_list, res]
    out_specs = pl.BlockSpec((tm, d), lambda i: (i, 0))
    out_shape = jax.ShapeDtypeStruct((m, d), F32)
    if proj is not None:
        g, wp = proj
        in_specs += [_const_spec((1, d)), _const_spec(wp.shape)]
        args += [g.reshape(1, d).astype(F32), wp]
        out_specs = [out_specs, pl.BlockSpec((tm, wp.shape[1]), lambda i: (i, 0))]
        out_shape = [out_shape, jax.ShapeDtypeStruct((m, wp.shape[1]), BF16)]
    return pl.pallas_call(
        functools.partial(_mm_res_kernel, n=n, proj=proj is not None),
        grid=(m // tm,),
        in_specs=in_specs,
        out_specs=out_specs,
        out_shape=out_shape,
        compiler_params=_cparams(("parallel",)),
        name="matmul_residual",
    )(*args)


FFN_CHUNK = 512


def _ffn_kernel(*refs, final_norm, pre):
    refs = list(refs)
    x_ref = refs.pop(0)
    x = x_ref[...]
    if pre:
        a_ref, wa_ref = refs.pop(0), refs.pop(0)
        x = x + jnp.dot(a_ref[...], wa_ref[...], preferred_element_type=F32)
    if final_norm:
        g_ref, wg_ref, wu_ref, wo_ref, gf_ref, o_ref, acc_ref = refs
    else:
        g_ref, wg_ref, wu_ref, wo_ref, o_ref, acc_ref = refs
    ms = jnp.mean(x * x, axis=-1, keepdims=True)
    xb = ((x * lax.rsqrt(ms + RMS_EPS)) * g_ref[...]).astype(BF16)
    hdim = wg_ref.shape[1]
    acc_ref[...] = x
    for c0 in range(0, hdim, FFN_CHUNK):
        c1 = min(c0 + FFN_CHUNK, hdim)
        gt = jnp.dot(xb, wg_ref[:, c0:c1], preferred_element_type=F32)
        up = jnp.dot(xb, wu_ref[:, c0:c1], preferred_element_type=F32)
        act = (_silu(gt) * up).astype(BF16)
        acc_ref[...] += jnp.dot(act, wo_ref[c0:c1, :], preferred_element_type=F32)
    y = acc_ref[...]
    if final_norm:
        ms2 = jnp.mean(y * y, axis=-1, keepdims=True)
        y = (y * lax.rsqrt(ms2 + RMS_EPS)) * gf_ref[...]
    o_ref[...] = y


def ffn(x, g, wg, wu, wo, g_final=None, pre=None, tm=512):
    m, d = x.shape
    tm = min(tm, m)
    assert m % tm == 0
    final_norm = g_final is not None
    in_specs = [pl.BlockSpec((tm, d), lambda i: (i, 0))]
    args = [x]
    if pre is not None:
        a, wa = pre
        in_specs += [pl.BlockSpec((tm, a.shape[1]), lambda i: (i, 0)), _const_spec(wa.shape)]
        args += [a, wa]
    in_specs += [_const_spec((1, d)), _const_spec(wg.shape), _const_spec(wu.shape), _const_spec(wo.shape)]
    args += [g.reshape(1, d).astype(F32), wg, wu, wo]
    if final_norm:
        in_specs.append(_const_spec((1, d)))
        args.append(g_final.reshape(1, d).astype(F32))
    return pl.pallas_call(
        functools.partial(_ffn_kernel, final_norm=final_norm, pre=pre is not None),
        grid=(m // tm,),
        in_specs=in_specs,
        out_specs=pl.BlockSpec((tm, d), lambda i: (i, 0)),
        out_shape=jax.ShapeDtypeStruct((m, d), F32),
        scratch_shapes=[pltpu.VMEM((tm, d), F32)],
        compiler_params=_cparams(("parallel",)),
        name="ffn",
    )(*args)


def _ssd_kernel(*refs, L, valid, has_h0):
    if has_h0:
        (xbc_ref, z_ref, dtr_ref, tail_ref, h0_ref, cw_ref, cb_ref, dtb_ref, al_ref, ale_ref,
         dsk_ref, gn_ref, y_ref, hout_ref, xpad_scr, hT_scr) = refs
    else:
        (xbc_ref, z_ref, dtr_ref, tail_ref, cw_ref, cb_ref, dtb_ref, al_ref, ale_ref,
         dsk_ref, gn_ref, y_ref, hout_ref, xpad_scr, hT_scr) = refs
    c = pl.program_id(1)
    nc = pl.num_programs(1)

    @pl.when(c == 0)
    def _():
        xpad_scr[0:8, :] = tail_ref[0]
        if has_h0:
            hT_scr[...] = h0_ref[0].T
        else:
            hT_scr[...] = jnp.zeros(hT_scr.shape, F32)

    @pl.when(c > 0)
    def _():
        xpad_scr[0:8, :] = xpad_scr[L:L + 8, :]

    xpad_scr[8:8 + L, :] = xbc_ref[0]
    acc = cb_ref[...]
    for j in range(CONV_K):
        acc = acc + xpad_scr[5 + j:5 + j + L, :] * cw_ref[j:j + 1, :]
    xc = _silu(acc)
    xs = xc[:, :SSM_WIDTH]
    gn_w = SSM_GROUPS * SSM_N
    bm = xc[:, SSM_WIDTH:SSM_WIDTH + gn_w]
    cm = xc[:, SSM_WIDTH + gn_w:]

    dt = _softplus(dtr_ref[0] + dtb_ref[...])
    if valid < L:
        dt = jnp.where(_iota((L, LANES), 0) < valid, dt, 0.0)
    a128 = -jnp.exp(al_ref[...])
    a_exp = -jnp.exp(ale_ref[...])

    expand = (_iota((LANES, SSM_WIDTH), 1) // SSM_P == _iota((LANES, SSM_WIDTH), 0)).astype(BF16)
    tril = (_iota((L, L), 0) >= _iota((L, L), 1))
    tril_b = tril.astype(BF16)

    dt_exp = _dot01_right(dt, expand)
    cs_exp = _dot01_left(tril_b, dt_exp * a_exp)
    cs = _dot01_left(tril_b, dt * a128)
    cs_t = cs.T

    last = cs_exp[L - 1:L, :]
    ecs = jnp.exp(cs_exp)
    wend = jnp.exp(last - cs_exp)
    dec = jnp.exp(last)
    xdt = xs * dt_exp
    xdt_b = xdt.astype(BF16)
    xw_b = (xdt * wend).astype(BF16)

    lane_lo = _iota((L, LANES), 1) < SSM_P
    gw = SSM_WIDTH // SSM_GROUPS
    y_parts = []
    for g in range(SSM_GROUPS):
        bg = bm[:, g * SSM_N:(g + 1) * SSM_N]
        cg_b = cm[:, g * SSM_N:(g + 1) * SSM_N].astype(BF16)
        cb = lax.dot_general(cg_b, bg.astype(BF16), NT_DIMS, preferred_element_type=F32)
        h_old = hT_scr[:, g * gw:(g + 1) * gw]
        y_state = jnp.dot(cg_b, h_old.astype(BF16), preferred_element_type=F32)
        pairs = []
        for i in range(gw // LANES):
            slab = g * (gw // LANES) + i
            xp = xdt_b[:, slab * LANES:(slab + 1) * LANES]
            halves = []
            for half in range(2):
                h = 2 * slab + half
                diff = cs[:, h:h + 1] - cs_t[h:h + 1, :]
                dm = jnp.exp(jnp.where(tril, diff, NEG_INF))
                sc = (cb * dm).astype(BF16)
                halves.append(jnp.dot(sc, xp, preferred_element_type=F32))
            pairs.append(jnp.where(lane_lo, halves[0], halves[1]))
        y_intra = jnp.concatenate(pairs, axis=1)
        y_parts.append(y_intra + y_state * ecs[:, g * gw:(g + 1) * gw])
        bg_t = bg.T.astype(BF16)
        upd = jnp.dot(bg_t, xw_b[:, g * gw:(g + 1) * gw], preferred_element_type=F32)
        hT_scr[:, g * gw:(g + 1) * gw] = h_old * dec[:, g * gw:(g + 1) * gw] + upd
    y = jnp.concatenate(y_parts, axis=1) + dsk_ref[...] * xs
    yg = y * _silu(z_ref[0])
    ms = jnp.mean(yg * yg, axis=-1, keepdims=True)
    y_ref[0] = ((yg * lax.rsqrt(ms + RMS_EPS)) * gn_ref[...]).astype(y_ref.dtype)

    @pl.when(c == nc - 1)
    def _():
        hout_ref[0] = hT_scr[...].T


def ssd(xbc, z, dtr, tail, h0, conv_w, conv_b, dt_bias, a_log, d_skip, ssm_norm, L, valid):
    b, t, _ = xbc.shape
    assert t % L == 0
    nc = t // L
    has_h0 = h0 is not None
    pad16 = lambda v: jnp.pad(v.astype(F32), (0, LANES - v.shape[0])).reshape(1, LANES)
    rep = lambda v: jnp.repeat(v.astype(F32), SSM_P).reshape(1, SSM_WIDTH)
    row = lambda last: pl.BlockSpec((1, L, last), lambda i, j: (i, j, 0))
    per_b = lambda s1, s2: pl.BlockSpec((1, s1, s2), lambda i, j: (i, 0, 0))
    in_specs = [row(CONV_DIM), row(SSM_WIDTH), row(LANES), per_b(8, CONV_DIM)]
    args = [xbc, z, dtr, tail]
    if has_h0:
        in_specs.append(per_b(SSM_WIDTH, SSM_N))
        args.append(h0)
    consts = [conv_w.astype(F32), conv_b.reshape(1, CONV_DIM).astype(F32), pad16(dt_bias), pad16(a_log),
              rep(a_log), rep(d_skip), ssm_norm.reshape(1, SSM_WIDTH).astype(F32)]
    in_specs += [_const_spec(cst.shape) for cst in consts]
    args += consts
    return pl.pallas_call(
        functools.partial(_ssd_kernel, L=L, valid=valid, has_h0=has_h0),
        grid=(b, nc),
        in_specs=in_specs,
        out_specs=[row(SSM_WIDTH), per_b(SSM_WIDTH, SSM_N)],
        out_shape=[jax.ShapeDtypeStruct((b, t, SSM_WIDTH), BF16),
                   jax.ShapeDtypeStruct((b, SSM_WIDTH, SSM_N), F32)],
        scratch_shapes=[pltpu.VMEM((L + 8, CONV_DIM), F32), pltpu.VMEM((SSM_N, SSM_WIDTH), F32)],
        compiler_params=_cparams(("parallel", "arbitrary")),
        name="ssd",
    )(*args)


GROUP = N_HEADS // N_KV


def _head_perm():
    order = []
    for slab in range(N_HEADS // 2):
        j, g = slab // GROUP, slab % GROUP
        order += [(2 * j) * GROUP + g, (2 * j + 1) * GROUP + g]
    return np.asarray(order)


HEAD_PERM = _head_perm()
COL_PERM = (HEAD_PERM[:, None] * HEAD_DIM + np.arange(HEAD_DIM)[None, :]).reshape(-1)
SLABS_PER_STEP = 4
HEADS_PER_STEP = 2 * SLABS_PER_STEP


LOG2E = math.log2(math.e)
Q_SCALE = LOG2E * HEAD_DIM ** -0.5


def _masked_q(q_ref, qm_scr):
    tq = q_ref.shape[1]
    lo = _iota((tq, LANES), 1) < HEAD_DIM
    for s in range(SLABS_PER_STEP):
        slab = q_ref[0, :, s * LANES:(s + 1) * LANES]
        zero = jnp.zeros_like(slab)
        qm_scr[2 * s] = jnp.where(lo, slab, zero)
        qm_scr[2 * s + 1] = jnp.where(lo, zero, slab)


def _write_heads(o_ref, acc, inv_l):
    tq = o_ref.shape[1]
    lo = _iota((tq, LANES), 1) < HEAD_DIM
    for s in range(SLABS_PER_STEP):
        a = acc[2 * s] * inv_l[2 * s]
        b = acc[2 * s + 1] * inv_l[2 * s + 1]
        o_ref[0, :, s * LANES:(s + 1) * LANES] = jnp.where(lo, a, b).astype(o_ref.dtype)


def _swa_prompt_kernel(q_ref, kp_ref, kc_ref, vp_ref, vc_ref, sink_ref, o_ref, qm_scr):
    i = pl.program_id(2)
    w = WINDOW
    _masked_q(q_ref, qm_scr)
    kcat = jnp.concatenate([kp_ref[0], kc_ref[0]], axis=0)
    vcat = jnp.concatenate([vp_ref[0], vc_ref[0]], axis=0)
    r = _iota((w, 2 * w), 0)
    col = _iota((w, 2 * w), 1)
    lim = jnp.where(i > 0, r, 2 * w)
    mask = ((col > lim) & (col < w)) | ((col >= w) & (col - w <= r))
    heads = range(HEADS_PER_STEP)
    scores = [jnp.where(mask, lax.dot_general(qm_scr[h], kcat, NT_DIMS, preferred_element_type=F32), NEG_INF)
              for h in heads]
    probs, inv_l = [], []
    for h in heads:
        sink = sink_ref[0, h:h + 1, :] * LOG2E
        m = jnp.maximum(jnp.max(scores[h], axis=-1, keepdims=True), sink)
        p = jnp.exp2(scores[h] - jnp.tile(m, (1, 2 * w // LANES)))
        inv_l.append(1.0 / (jnp.sum(p, axis=-1, keepdims=True) + jnp.exp2(sink - m)))
        probs.append(p.astype(BF16))
    acc = [jnp.dot(probs[h], vcat, preferred_element_type=F32) for h in heads]
    _write_heads(o_ref, acc, inv_l)


def swa_prompt(q, kb, vb, sink_perm):
    b, t, _ = q.shape
    w = WINDOW
    nb = t // w
    npair = N_KV // 2
    sink_arr = jnp.broadcast_to(sink_perm.astype(F32).reshape(npair, HEADS_PER_STEP, 1), (npair, HEADS_PER_STEP, LANES))
    qspec = pl.BlockSpec((1, w, SLABS_PER_STEP * LANES), lambda bi, j, i: (bi, i, j))
    cur = pl.BlockSpec((1, w, LANES), lambda bi, j, i: (bi, i, j))
    prev = pl.BlockSpec((1, w, LANES), lambda bi, j, i: (bi, jnp.maximum(i - 1, 0), j))
    return pl.pallas_call(
        _swa_prompt_kernel,
        grid=(b, npair, nb),
        in_specs=[qspec, prev, cur, prev, cur, pl.BlockSpec((1, HEADS_PER_STEP, LANES), lambda bi, j, i: (j, 0, 0))],
        out_specs=qspec,
        out_shape=jax.ShapeDtypeStruct(q.shape, BF16),
        scratch_shapes=[pltpu.VMEM((HEADS_PER_STEP, w, LANES), BF16)],
        compiler_params=_cparams(("parallel", "parallel", "arbitrary")),
        name="swa_prompt",
    )(q, kb, kb, vb, vb, sink_arr)


def _cumsum_t_kernel(lf_ref, o_ref, carry_scr, *, rows):
    c = pl.program_id(1)

    @pl.when(c == 0)
    def _():
        carry_scr[...] = jnp.zeros(carry_scr.shape, F32)

    L = lf_ref.shape[1]
    tril_b = (_iota((L, L), 0) >= _iota((L, L), 1)).astype(BF16)
    cs = _dot01_left(tril_b, lf_ref[0]) + carry_scr[...]
    carry_scr[...] = cs[L - 1:L, :]
    o_ref[0] = cs.T[0:rows, :] * LOG2E


def cumsum_t(lf, rows):
    b, t, _ = lf.shape
    L = min(512, t)
    return pl.pallas_call(
        functools.partial(_cumsum_t_kernel, rows=rows),
        grid=(b, t // L),
        in_specs=[pl.BlockSpec((1, L, LANES), lambda i, j: (i, j, 0))],
        out_specs=pl.BlockSpec((1, rows, L), lambda i, j: (i, 0, j)),
        out_shape=jax.ShapeDtypeStruct((b, rows, t), F32),
        scratch_shapes=[pltpu.VMEM((1, LANES), F32)],
        compiler_params=_cparams(("parallel", "arbitrary")),
        name="cumsum_t",
    )(lf)


def _fox_prompt_kernel(qi_ref, ki_ref, q_ref, k_ref, v_ref, c_ref, o_ref, qm_scr, m_scr, l_scr, acc_scr, *, tq, tk):
    step = pl.program_id(2)
    qi = qi_ref[step]
    ki = ki_ref[step]
    last_k = (qi * tq + tq - 1) // tk

    @pl.when(ki == 0)
    def _():
        _masked_q(q_ref, qm_scr)
        m_scr[...] = jnp.full(m_scr.shape, NEG_INF, F32)
        l_scr[...] = jnp.zeros(l_scr.shape, F32)
        acc_scr[...] = jnp.zeros(acc_scr.shape, F32)

    def body(masked):
        kb = k_ref[0]
        vb = v_ref[0]
        heads = range(HEADS_PER_STEP)
        if masked:
            row = qi * tq + _iota((tq, tk), 0)
            col = ki * tk + _iota((tq, tk), 1)
            keep = col <= row
        scores = []
        for h in heads:
            s = lax.dot_general(qm_scr[h], kb, NT_DIMS, preferred_element_type=F32) - c_ref[0, h:h + 1, :]
            scores.append(jnp.where(keep, s, NEG_INF) if masked else s)
        probs, alphas = [], []
        for h in heads:
            m_prev = m_scr[h]
            m_new = jnp.maximum(m_prev, jnp.max(scores[h], axis=-1, keepdims=True))
            alpha = jnp.exp2(m_prev - m_new)
            p = jnp.exp2(scores[h] - jnp.tile(m_new, (1, tk // LANES)))
            l_scr[h] = alpha * l_scr[h] + jnp.sum(p, axis=-1, keepdims=True)
            m_scr[h] = m_new
            probs.append(p.astype(BF16))
            alphas.append(alpha)
        for h in heads:
            acc_scr[h] = alphas[h] * acc_scr[h] + jnp.dot(probs[h], vb, preferred_element_type=F32)

    needs_mask = (ki + 1) * tk - 1 > qi * tq

    @pl.when(needs_mask)
    def _():
        body(True)

    @pl.when(jnp.logical_not(needs_mask))
    def _():
        body(False)

    @pl.when(ki == last_k)
    def _():
        _write_heads(o_ref, [acc_scr[h] for h in range(HEADS_PER_STEP)],
                     [1.0 / l_scr[h] for h in range(HEADS_PER_STEP)])


def fox_prompt(q, kb, vb, cum_t, tq=512, tk=512):
    b, t, _ = q.shape
    tq, tk = min(tq, t), min(tk, t)
    npair = N_KV // 2
    qi_l, ki_l = [], []
    for qi in range(t // tq):
        for ki in range((qi * tq + tq - 1) // tk + 1):
            qi_l.append(qi)
            ki_l.append(ki)
    qi_arr = jnp.asarray(qi_l, jnp.int32)
    ki_arr = jnp.asarray(ki_l, jnp.int32)
    qspec = pl.BlockSpec((1, tq, SLABS_PER_STEP * LANES), lambda bi, j, s, qi, ki: (bi, qi[s], j))
    kspec = pl.BlockSpec((1, tk, LANES), lambda bi, j, s, qi, ki: (bi, ki[s], j))
    cspec = pl.BlockSpec((1, HEADS_PER_STEP, tk), lambda bi, j, s, qi, ki: (bi, j, ki[s]))
    grid_spec = pltpu.PrefetchScalarGridSpec(
        num_scalar_prefetch=2,
        grid=(b, npair, len(qi_l)),
        in_specs=[qspec, kspec, kspec, cspec],
        out_specs=qspec,
        scratch_shapes=[pltpu.VMEM((HEADS_PER_STEP, tq, LANES), BF16),
                        pltpu.VMEM((HEADS_PER_STEP, tq, LANES), F32),
                        pltpu.VMEM((HEADS_PER_STEP, tq, LANES), F32),
                        pltpu.VMEM((HEADS_PER_STEP, tq, LANES), F32)],
    )
    return pl.pallas_call(
        functools.partial(_fox_prompt_kernel, tq=tq, tk=tk),
        grid_spec=grid_spec,
        out_shape=jax.ShapeDtypeStruct(q.shape, BF16),
        compiler_params=_cparams(("parallel", "parallel", "arbitrary")),
        name="fox_prompt",
    )(qi_arr, ki_arr, q, kb, vb, cum_t)


def _xattn_kernel(q_ref, k_ref, v_ref, o_ref, *, nb, mem, interleaved):
    scale = LOG2E * XA_DIM ** -0.5
    pairs = [(bi, h) for bi in range(nb) for h in range(XA_HEADS)]
    lanes = lambda h: slice(h * XA_DIM, (h + 1) * XA_DIM)

    def kv_head(ref, bi, h):
        if interleaved:
            return ref[bi, pl.ds(h, mem, stride=XA_HEADS), :].astype(BF16)
        return ref[bi, :, lanes(h)].astype(BF16)

    scores = [lax.dot_general(q_ref[bi, :, lanes(h)], kv_head(k_ref, bi, h), NT_DIMS,
                              preferred_element_type=F32) * scale for bi, h in pairs]
    probs, inv_l = [], []
    for s in scores:
        p = jnp.exp2(s - jnp.max(s, axis=-1, keepdims=True))
        inv_l.append(1.0 / jnp.sum(p, axis=-1, keepdims=True))
        probs.append(p.astype(BF16))
    for (bi, h), p, il in zip(pairs, probs, inv_l):
        o = jnp.dot(p, kv_head(v_ref, bi, h), preferred_element_type=F32) * il
        o_ref[bi, :, lanes(h)] = o.astype(o_ref.dtype)


def xattn(q, mk, mv, tq, nb, kv_first=0):
    b, t, _ = q.shape
    interleaved = mk.shape[2] == XA_DIM
    mem = mk.shape[1] // XA_HEADS if interleaved else mk.shape[1]
    assert kv_first % nb == 0
    off = kv_first // nb
    qspec = pl.BlockSpec((nb, tq, XA_WIDTH), lambda i, j: (i, j, 0))
    mspec = pl.BlockSpec((nb,) + mk.shape[1:], lambda i, j: (i + off, 0, 0))
    return pl.pallas_call(
        functools.partial(_xattn_kernel, nb=nb, mem=mem, interleaved=interleaved),
        grid=(b // nb, t // tq),
        in_specs=[qspec, mspec, mspec],
        out_specs=qspec,
        out_shape=jax.ShapeDtypeStruct(q.shape, BF16),
        compiler_params=_cparams(("parallel", "arbitrary")),
        name="xattn",
    )(q, mk, mv)


def _block_diag_q(q, nb, t):
    q4 = q.reshape(nb, t * N_HEADS, HEAD_DIM)
    qt = jnp.tile(q4, (1, 1, N_KV))
    row_kv = (np.arange(t * N_HEADS) % N_HEADS) // GROUP
    keep = jnp.asarray(row_kv[:, None] == (np.arange(N_KV * HEAD_DIM) // HEAD_DIM)[None, :])
    return jnp.where(keep[None], qt * jnp.asarray(HEAD_DIM ** -0.5, BF16), jnp.zeros((), BF16))


def _diag_extract(acc, rows):
    row_kv = (_iota((rows, HEAD_DIM), 0) % N_HEADS) // GROUP
    out = jnp.zeros((rows, HEAD_DIM), F32)
    for kv in range(N_KV):
        out = out + jnp.where(row_kv == kv, acc[:, kv * HEAD_DIM:(kv + 1) * HEAD_DIM], 0.0)
    return out


NEW_PAD = 16


def _swa_sample_kernel(q_ref, ck_ref, cv_ref, kn_ref, vn_ref, sink_ref, o_ref, ok_ref, ov_ref, *, nb, t):
    w = WINDOW
    rows = t * N_HEADS
    kvw = N_KV * HEAD_DIM
    tok = _iota((rows, 2 * w), 0) // N_HEADS
    col = _iota((rows, 2 * w), 1)
    mask = ((col > tok) & (col < w)) | ((col >= w) & (col - w <= tok))
    sink = sink_ref[:, 0:1]
    zpad = jnp.zeros((w - NEW_PAD, kvw), BF16)
    for bi in range(nb):
        ck = ck_ref[bi]
        cv = cv_ref[bi]
        kn = kn_ref[bi]
        vn = vn_ref[bi]
        kcat = jnp.concatenate([ck.astype(BF16), kn.astype(BF16), zpad], axis=0)
        vcat = jnp.concatenate([cv.astype(BF16), vn.astype(BF16), zpad], axis=0)
        s = lax.dot_general(q_ref[bi], kcat, NT_DIMS, preferred_element_type=F32)
        s = jnp.where(mask, s, NEG_INF)
        m = jnp.maximum(jnp.max(s, axis=-1, keepdims=True), sink)
        p = jnp.exp(s - m)
        l = jnp.sum(p, axis=-1, keepdims=True) + jnp.exp(sink - m)
        acc = jnp.dot(p.astype(BF16), vcat, preferred_element_type=F32)
        o_ref[bi] = (_diag_extract(acc, rows) * (1.0 / l)).astype(o_ref.dtype)
        ok_ref[bi, 0:w - t, :] = ck[t:w, :]
        ov_ref[bi, 0:w - t, :] = cv[t:w, :]
        ok_ref[bi, w - t:w, :] = kn[0:t, :]
        ov_ref[bi, w - t:w, :] = vn[0:t, :]


def swa_sample(qbd, ck, cv, kn, vn, sink, t, nb=8):
    b = qbd.shape[0]
    w = WINDOW
    rows = t * N_HEADS
    sink_rows = jnp.broadcast_to(jnp.tile(sink.astype(F32), t).reshape(rows, 1), (rows, LANES))
    bspec = lambda s1, s2: pl.BlockSpec((nb, s1, s2), lambda i: (i, 0, 0))
    kvw = N_KV * HEAD_DIM
    return pl.pallas_call(
        functools.partial(_swa_sample_kernel, nb=nb, t=t),
        grid=(b // nb,),
        in_specs=[bspec(rows, kvw), bspec(w, kvw), bspec(w, kvw), bspec(NEW_PAD, kvw), bspec(NEW_PAD, kvw),
                  _const_spec((rows, LANES))],
        out_specs=[bspec(rows, HEAD_DIM), bspec(w, kvw), bspec(w, kvw)],
        out_shape=[jax.ShapeDtypeStruct((b, rows, HEAD_DIM), BF16),
                   jax.ShapeDtypeStruct((b, w, kvw), F32), jax.ShapeDtypeStruct((b, w, kvw), F32)],
        compiler_params=_cparams(("parallel",)),
        name="swa_sample",
    )(qbd, ck, cv, kn, vn, sink_rows)


FOX_PAGES_PER_STEP = 16


def _fox_sample_kernel(pt_ref, q_ref, kn_ref, vn_ref, lfn_ref, kc_hbm, vc_hbm, lf_hbm, o_ref,
                       kbuf, vbuf, lbuf, sem, carry_scr, m_scr, l_scr, acc_scr,
                       *, t, n_pages, pg):
    b = pl.program_id(0)
    c = pl.program_id(1)
    nch = pl.num_programs(1)
    step = b * nch + c
    total = pl.num_programs(0) * nch
    rows = t * N_HEADS
    nh = N_HEADS

    def copies(stp, slot):
        base = (stp // nch) * n_pages + (stp % nch) * pg
        out = []
        for p in range(pg):
            page = pt_ref[base + p]
            out.append(pltpu.make_async_copy(kc_hbm.at[page], kbuf.at[slot, :, pl.ds(p * PAGE, PAGE)], sem.at[slot, 0]))
            out.append(pltpu.make_async_copy(vc_hbm.at[page], vbuf.at[slot, :, pl.ds(p * PAGE, PAGE)], sem.at[slot, 1]))
            out.append(pltpu.make_async_copy(lf_hbm.at[page], lbuf.at[slot, pl.ds(p * nh, nh)], sem.at[slot, 2]))
        return out

    slot = step % 2

    @pl.when(step == 0)
    def _():
        for cp in copies(step, slot):
            cp.start()

    @pl.when(step + 1 < total)
    def _():
        for cp in copies(step + 1, 1 - slot):
            cp.start()

    @pl.when(c == 0)
    def _():
        carry_scr[...] = jnp.zeros(carry_scr.shape, F32)
        m_scr[...] = jnp.full(m_scr.shape, NEG_INF, F32)
        l_scr[...] = jnp.zeros(l_scr.shape, F32)
        acc_scr[...] = jnp.zeros(acc_scr.shape, F32)

    for cp in copies(step, slot):
        cp.wait()

    q = q_ref[0]
    upper_b = (_iota((PAGE, PAGE), 0) <= _iota((PAGE, PAGE), 1)).astype(BF16)

    def online(s, pv):
        m_prev = m_scr[...]
        m_new = jnp.maximum(m_prev, jnp.max(s, axis=-1, keepdims=True))
        alpha = jnp.exp(m_prev - m_new)
        p = jnp.exp(s - m_new)
        l_scr[...] = alpha * l_scr[...] + jnp.sum(p, axis=-1, keepdims=True)
        acc_scr[...] = alpha * acc_scr[...] + pv(p.astype(BF16))
        m_scr[...] = m_new

    lf = lbuf[slot]
    cl = _dot01_right(lf, upper_b)
    tot = jnp.broadcast_to(cl[:, PAGE - 1:PAGE], cl.shape)
    r = _iota((pg * nh, pg * nh), 0)
    cc = _iota((pg * nh, pg * nh), 1)
    earlier_b = ((r % nh == cc % nh) & (cc // nh < r // nh)).astype(BF16)
    cum = cl + _dot01_left(earlier_b, tot) + jnp.concatenate([carry_scr[...]] * pg, axis=0)
    carry_new = jnp.broadcast_to(cum[(pg - 1) * nh:pg * nh, PAGE - 1:PAGE], (nh, LANES))
    carry_scr[...] = carry_new
    bias16 = jnp.concatenate([cum[p * nh:(p + 1) * nh, :] for p in range(pg)], axis=1)
    bias = jnp.concatenate([bias16] * t, axis=0)

    kc = kbuf[slot].astype(BF16)
    vc = vbuf[slot].astype(BF16)
    s = jnp.dot(q, kc, preferred_element_type=F32) - bias
    online(s, lambda p: lax.dot_general(p, vc, NT_DIMS, preferred_element_type=F32))

    @pl.when(c == nch - 1)
    def _():
        zpad = jnp.zeros((PAGE - NEW_PAD, N_KV * HEAD_DIM), BF16)
        knew = jnp.concatenate([kn_ref[0].astype(BF16), zpad], axis=0)
        vnew = jnp.concatenate([vn_ref[0].astype(BF16), zpad], axis=0)
        cum_n = _dot01_right(lfn_ref[0], upper_b) + carry_new
        sn = lax.dot_general(q, knew, NT_DIMS, preferred_element_type=F32)
        sn = sn - jnp.concatenate([cum_n] * t, axis=0)
        tok = _iota((rows, PAGE), 0) // nh
        sn = jnp.where(_iota((rows, PAGE), 1) <= tok, sn, NEG_INF)
        online(sn, lambda p: jnp.dot(p, vnew, preferred_element_type=F32))
        o_ref[0] = (_diag_extract(acc_scr[...], rows) * (1.0 / l_scr[...])).astype(o_ref.dtype)


def fox_sample(page_table, qbd, kn, vn, lfn_t, kcache, vcache, lf_t_cache, t, pg=FOX_PAGES_PER_STEP):
    b, n_pages = page_table.shape
    pg = min(pg, n_pages)
    assert n_pages % pg == 0
    rows = t * N_HEADS
    kvw = N_KV * HEAD_DIM
    bspec = lambda s1, s2: pl.BlockSpec((1, s1, s2), lambda i, j, pt: (i, 0, 0))
    any_spec = pl.BlockSpec(memory_space=pl.ANY)
    grid_spec = pltpu.PrefetchScalarGridSpec(
        num_scalar_prefetch=1,
        grid=(b, n_pages // pg),
        in_specs=[bspec(rows, kvw), bspec(NEW_PAD, kvw), bspec(NEW_PAD, kvw), bspec(N_HEADS, LANES),
                  any_spec, any_spec, any_spec],
        out_specs=bspec(rows, HEAD_DIM),
        scratch_shapes=[pltpu.VMEM((2, kvw, pg * PAGE), F32), pltpu.VMEM((2, kvw, pg * PAGE), F32),
                        pltpu.VMEM((2, pg * N_HEADS, LANES), F32), pltpu.SemaphoreType.DMA((2, 3)),
                        pltpu.VMEM((N_HEADS, LANES), F32),
                        pltpu.VMEM((rows, 1), F32), pltpu.VMEM((rows, 1), F32), pltpu.VMEM((rows, kvw), F32)],
    )
    return pl.pallas_call(
        functools.partial(_fox_sample_kernel, t=t, n_pages=n_pages, pg=pg),
        grid_spec=grid_spec,
        out_shape=jax.ShapeDtypeStruct((b, rows, HEAD_DIM), BF16),
        compiler_params=_cparams(("arbitrary", "arbitrary")),
        name="fox_sample",
    )(page_table.reshape(-1), qbd, kn, vn, lfn_t, kcache, vcache, lf_t_cache)


def _pad_cols(w, width):
    return jnp.pad(w, ((0, 0), (0, width - w.shape[1])))


def kernel(x_prompt, x_sample, state_ssm, state_conv, cache_swa_k, cache_swa_v, cache_fox_k, cache_fox_v,
           cache_fox_logf, cache_mem_k, cache_mem_v, page_table, mem_prompt, norm_mix, norm_xa, norm_mem,
           norm_ffn, w_in_even, conv_w, conv_b, dt_bias, a_log, d_skip, ssm_norm, swa_sink, w_out_even,
           w_in_odd, fox_fb, w_out_odd, w_xq, w_xk, w_xv, w_xo, w_ffn_in, w_ffn_out, norm_final):
    bp, seq, d = x_prompt.shape
    bs, ts, _ = x_sample.shape
    depth = norm_mix.shape[0]
    mem_len = mem_prompt.shape[1]
    kvw = N_KV * HEAD_DIM
    qw = N_HEADS * HEAD_DIM
    xp = x_prompt.reshape(bp * seq, d)
    xs = x_sample.reshape(bs * ts, d)
    mem2 = mem_prompt.reshape(bp * mem_len, d)
    outs = {k: [] for k in ("p_ssm", "p_conv", "p_swk", "p_swv", "p_fk", "p_fv", "p_fl", "p_mk", "p_mv",
                            "s_ssm", "s_conv", "s_swk", "s_swv", "s_fk", "s_fv", "s_fl")}

    for l in range(depth):
        li = l // 2
        if l % 2 == 0:
            w = w_in_even[li]
            o_z, o_xbc, o_dt = 0, SSM_WIDTH, SSM_WIDTH + CONV_DIM
            o_q = o_dt + SSM_HEADS
            o_k, o_v = o_q + qw, o_q + qw + kvw
            wz, wxbc, wdt = w[:, o_z:o_xbc], w[:, o_xbc:o_dt], _pad_cols(w[:, o_dt:o_q], LANES)
            wq, wk, wv = w[:, o_q:o_k], w[:, o_k:o_v], w[:, o_v:o_v + kvw]
            c_xbc, c_q, c_k, c_v, c_dt = SSM_WIDTH, SSM_WIDTH + CONV_DIM, SSM_WIDTH + CONV_DIM + qw, \
                SSM_WIDTH + CONV_DIM + qw + kvw, SSM_WIDTH + CONV_DIM + qw + 2 * kvw
            n_tot = c_dt + LANES
            wo = w_out_even[li]
            wo_ssm = wo[:SSM_WIDTH].astype(BF16)
            wo_att = wo[SSM_WIDTH:]
            ssm_params = (conv_w[li], conv_b[li], dt_bias[li], a_log[li], d_skip[li], ssm_norm[li])

            w_p = jnp.concatenate([wz, wxbc, wq[:, COL_PERM], wk, wv, wdt], axis=1).astype(BF16)
            segs_p = [(0, c_xbc, (F32,), None), (c_xbc, c_q, (F32,), None), (c_q, c_k, (BF16,), Q_SCALE),
                      (c_k, c_v, (F32, BF16), None), (c_v, c_dt, (F32, BF16), None), (c_dt, n_tot, (F32,), None)]
            z, xbc, q, k, kb, v, vb, dtr = norm_proj(xp, norm_mix[l], w_p, segs_p)
            xbc3 = xbc.reshape(bp, seq, CONV_DIM)
            y_ssm, h = ssd(xbc3, z.reshape(bp, seq, SSM_WIDTH), dtr.reshape(bp, seq, LANES),
                           jnp.zeros((bp, 8, CONV_DIM), F32), None, *ssm_params, L=SSM_CHUNK, valid=SSM_CHUNK)
            y_att = swa_prompt(q.reshape(bp, seq, qw), kb.reshape(bp, seq, kvw), vb.reshape(bp, seq, kvw),
                               swa_sink[li][HEAD_PERM])
            mix_p = ([y_ssm.reshape(bp * seq, SSM_WIDTH), y_att.reshape(bp * seq, qw)],
                     [wo_ssm, wo_att[COL_PERM].astype(BF16)])
            last_w = lambda a: a.reshape(bp, seq, kvw)[:, seq - WINDOW:].reshape(bp, WINDOW, N_KV, HEAD_DIM)
            outs["p_ssm"].append(h.reshape(bp, SSM_HEADS, SSM_P, SSM_N))
            outs["p_conv"].append(xbc3[:, seq - (CONV_K - 1):])
            outs["p_swk"].append(last_w(k))
            outs["p_swv"].append(last_w(v))

            w_s = jnp.concatenate([wz, wxbc, wq, wk, wv, wdt], axis=1).astype(BF16)
            segs_s = [(0, c_xbc, (F32,), None), (c_xbc, c_q, (F32,), None), (c_q, c_k, (BF16,), None),
                      (c_k, c_v, (F32,), None), (c_v, c_dt, (F32,), None), (c_dt, n_tot, (F32,), None)]
            z, xbc, q, k, v, dtr = norm_proj(xs, norm_mix[l], w_s, segs_s)
            ls = NEW_PAD
            pad_t = lambda a: jnp.pad(a.reshape(bs, ts, a.shape[-1]), ((0, 0), (0, ls - ts), (0, 0)))
            xbc3 = xbc.reshape(bs, ts, CONV_DIM)
            tail = jnp.pad(state_conv[li].astype(F32), ((0, 0), (8 - (CONV_K - 1), 0), (0, 0)))
            y_ssm, h = ssd(pad_t(xbc), pad_t(z), pad_t(dtr), tail, state_ssm[li].reshape(bs, SSM_WIDTH, SSM_N),
                           *ssm_params, L=ls, valid=ts)
            y_ssm = y_ssm[:, :ts].reshape(bs * ts, SSM_WIDTH)
            kn, vn = pad_t(k), pad_t(v)
            o_att, new_k, new_v = swa_sample(_block_diag_q(q, bs, ts), cache_swa_k[li].reshape(bs, WINDOW, kvw),
                                             cache_swa_v[li].reshape(bs, WINDOW, kvw), kn, vn, swa_sink[li], ts)
            mix_s = ([y_ssm, o_att.reshape(bs * ts, qw)], [wo_ssm, wo_att.astype(BF16)])
            outs["s_ssm"].append(h.reshape(bs, SSM_HEADS, SSM_P, SSM_N))
            outs["s_conv"].append(jnp.concatenate([state_conv[li].astype(F32), xbc3], axis=1)[:, -(CONV_K - 1):])
            outs["s_swk"].append(new_k.reshape(bs, WINDOW, N_KV, HEAD_DIM))
            outs["s_swv"].append(new_v.reshape(bs, WINDOW, N_KV, HEAD_DIM))
        else:
            w = w_in_odd[li]
            wq, wk, wv, wf = w[:, :qw], w[:, qw:qw + kvw], w[:, qw + kvw:qw + 2 * kvw], w[:, qw + 2 * kvw:]
            c_k, c_v, c_f = qw, qw + kvw, qw + 2 * kvw
            n_tot = c_f + LANES
            fb = fox_fb[li].astype(F32)

            wf_p = _pad_cols(jnp.concatenate([wf[:, HEAD_PERM], wf], axis=1), LANES)
            fb_p = jnp.pad(jnp.concatenate([fb[HEAD_PERM], fb]), (0, LANES - 2 * N_HEADS)).reshape(1, LANES)
            w_p = jnp.concatenate([wq[:, COL_PERM], wk, wv, wf_p], axis=1).astype(BF16)
            segs_p = [(0, c_k, (BF16,), Q_SCALE), (c_k, c_v, ("T", BF16), None), (c_v, c_f, ("T", BF16), None),
                      (c_f, n_tot, (F32,), "logsig")]
            q, k_t, kb, v_t, vb, lf = norm_proj(xp, norm_mix[l], w_p, segs_p, aux=fb_p, rows_per_batch=seq)
            lf3 = lf.reshape(bp, seq, LANES)
            cum_t = cumsum_t(lf3, N_HEADS)
            y = fox_prompt(q.reshape(bp, seq, qw), kb.reshape(bp, seq, kvw), vb.reshape(bp, seq, kvw), cum_t)
            mix_p = ([y.reshape(bp * seq, qw)], [w_out_odd[li][COL_PERM].astype(BF16)])
            untr = lambda a: jnp.transpose(a.reshape(bp, N_KV, HEAD_DIM, seq), (0, 3, 1, 2))
            outs["p_fk"].append(untr(k_t))
            outs["p_fv"].append(untr(v_t))
            outs["p_fl"].append(lf3[:, :, N_HEADS:2 * N_HEADS])

            w_s = jnp.concatenate([wq, wk, wv, _pad_cols(wf, LANES)], axis=1).astype(BF16)
            fb_s = jnp.pad(fb, (0, LANES - N_HEADS)).reshape(1, LANES)
            segs_s = [(0, c_k, (BF16,), None), (c_k, c_v, (F32,), None), (c_v, c_f, (F32,), None),
                      (c_f, n_tot, (F32,), "logsig")]
            q, k, v, lf = norm_proj(xs, norm_mix[l], w_s, segs_s, aux=fb_s)
            lf_new = lf[:, :N_HEADS].reshape(bs, ts, N_HEADS)
            pad_t = lambda a: jnp.pad(a.reshape(bs, ts, a.shape[-1]), ((0, 0), (0, NEW_PAD - ts), (0, 0)))
            lfn_t = jnp.pad(jnp.swapaxes(lf_new, 1, 2), ((0, 0), (0, 0), (0, LANES - ts)))
            pool = cache_fox_k.shape[1]
            page_t = lambda c: jnp.transpose(c[li].astype(F32), (0, 2, 3, 1)).reshape(pool, kvw, PAGE)
            y = fox_sample(page_table, _block_diag_q(q, bs, ts), pad_t(k), pad_t(v), lfn_t,
                           page_t(cache_fox_k), page_t(cache_fox_v),
                           jnp.swapaxes(cache_fox_logf[li].astype(F32), 1, 2), ts)
            mix_s = ([y.reshape(bs * ts, qw)], [w_out_odd[li].astype(BF16)])
            outs["s_fk"].append(k.reshape(bs, ts, N_KV, HEAD_DIM))
            outs["s_fv"].append(v.reshape(bs, ts, N_KV, HEAD_DIM))
            outs["s_fl"].append(lf_new)

        wkv = jnp.concatenate([w_xk[l], w_xv[l]], axis=1).astype(BF16)
        mk, mv = norm_proj(mem2, norm_mem[l], wkv, [(0, XA_WIDTH, (F32,), None), (XA_WIDTH, 2 * XA_WIDTH, (F32,), None)])
        wq_b = w_xq[l].astype(BF16)
        wo_b = w_xo[l].astype(BF16)
        xp, qx = matmul_residual(*mix_p, xp, proj=(norm_xa[l], wq_b))
        ox_p = xattn(qx.reshape(bp, seq, XA_WIDTH), mk.reshape(bp, mem_len, XA_WIDTH),
                     mv.reshape(bp, mem_len, XA_WIDTH), tq=512, nb=1).reshape(bp * seq, XA_WIDTH)
        xs, qx = matmul_residual(*mix_s, xs, proj=(norm_xa[l], wq_b))
        qx = jnp.pad(qx.reshape(bs, ts, XA_WIDTH), ((0, 0), (0, NEW_PAD - ts), (0, 0)))
        all_layers = lambda c: c.astype(F32).reshape(depth * bs, mem_len * XA_HEADS, XA_DIM)
        ox_s = xattn(qx, all_layers(cache_mem_k), all_layers(cache_mem_v), tq=NEW_PAD, nb=4, kv_first=l * bs)
        ox_s = ox_s[:, :ts].reshape(bs * ts, XA_WIDTH)
        outs["p_mk"].append(mk.reshape(bp, mem_len, XA_HEADS, XA_DIM))
        outs["p_mv"].append(mv.reshape(bp, mem_len, XA_HEADS, XA_DIM))

        wg = w_ffn_in[l][:, :FFN_H].astype(BF16)
        wu = w_ffn_in[l][:, FFN_H:].astype(BF16)
        wd = w_ffn_out[l].astype(BF16)
        gf = norm_final if l == depth - 1 else None
        xp = ffn(xp, norm_ffn[l], wg, wu, wd, g_final=gf, pre=(ox_p, wo_b))
        xs = ffn(xs, norm_ffn[l], wg, wu, wd, g_final=gf, pre=(ox_s, wo_b))

    st = lambda key: jnp.stack(outs[key])
    return (xp.reshape(bp, seq, d), xs.reshape(bs, ts, d),
            st("p_ssm"), st("p_conv"), st("p_swk"), st("p_swv"), st("p_fk"), st("p_fv"), st("p_fl"),
            st("p_mk"), st("p_mv"), st("s_ssm"), st("s_conv"), st("s_swk"), st("s_swv"),
            st("s_fk"), st("s_fv"), st("s_fl"))
```

```python
import functools
import math

import numpy as np
import jax
import jax.numpy as jnp
from jax import lax
from jax.experimental import pallas as pl
from jax.experimental.pallas import tpu as pltpu

F32 = jnp.float32
BF16 = jnp.bfloat16

D_MODEL = 1024
HEAD_DIM = 64
N_HEADS = 16
N_KV = 4
SSM_HEADS = 16
SSM_P = 64
SSM_WIDTH = 1024
SSM_N = 128
SSM_GROUPS = 2
SSM_CHUNK = 128
CONV_K = 4
CONV_DIM = 1536
WINDOW = 128
PAGE = 128
XA_HEADS = 4
XA_DIM = 128
XA_WIDTH = 512
FFN_H = 2816
RMS_EPS = 1e-6
NEG_INF = -1e30

V7X_VMEM_BYTES = 64 * 1024 * 1024
VMEM_LIMIT = V7X_VMEM_BYTES - 8 * 1024 * 1024
LANES = 128

NT_DIMS = (((1,), (1,)), ((), ()))


def _cparams(sem):
    return pltpu.CompilerParams(dimension_semantics=sem, vmem_limit_bytes=VMEM_LIMIT)


def _const_spec(shape):
    nd = len(shape)
    return pl.BlockSpec(shape, lambda *_: (0,) * nd, pipeline_mode=pl.Buffered(1))


def _softplus(x):
    return jnp.maximum(x, 0.0) + jnp.log1p(jnp.exp(-jnp.abs(x)))


def _silu(x):
    return x * (1.0 / (1.0 + jnp.exp(-x)))


def _split3(x):
    hi = x.astype(BF16)
    r1 = x - hi.astype(F32)
    mid = r1.astype(BF16)
    lo = (r1 - mid.astype(F32)).astype(BF16)
    return hi, mid, lo


def _dot01_left(m01, x):
    hi, mid, lo = _split3(x)
    d = lambda b: jnp.dot(m01, b, preferred_element_type=F32)
    return (d(lo) + d(mid)) + d(hi)


def _dot01_right(x, m01):
    hi, mid, lo = _split3(x)
    d = lambda a: jnp.dot(a, m01, preferred_element_type=F32)
    return (d(lo) + d(mid)) + d(hi)


def _iota(shape, dim):
    return lax.broadcasted_iota(jnp.int32, shape, dim)


def _norm_proj_kernel(*refs, segs, has_aux):
    x_ref, g_ref, w_ref = refs[:3]
    pos = 3
    aux_ref = None
    if has_aux:
        aux_ref = refs[3]
        pos = 4
    out_refs = refs[pos:]
    x = x_ref[...]
    ms = jnp.mean(x * x, axis=-1, keepdims=True)
    xn = (x * lax.rsqrt(ms + RMS_EPS)) * g_ref[...]
    xb = xn.astype(BF16)
    oi = 0
    for (c0, c1, dtypes, act) in segs:
        y = jnp.dot(xb, w_ref[:, c0:c1], preferred_element_type=F32)
        if act == "logsig":
            y = -_softplus(-(y + aux_ref[...]))
        elif act is not None:
            y = y * act
        for dt in dtypes:
            if dt == "T":
                out_refs[oi][0] = y.T
            else:
                out_refs[oi][...] = y.astype(dt)
            oi += 1


def norm_proj(x, g, w, segs, aux=None, tm=512, rows_per_batch=None):
    m, d = x.shape
    tm = min(tm, m)
    assert m % tm == 0
    n = w.shape[1]
    tiles_per_b = (rows_per_batch or m) // tm
    in_specs = [pl.BlockSpec((tm, d), lambda i: (i, 0)), _const_spec((1, d)), _const_spec((d, n))]
    args = [x, g.reshape(1, d).astype(F32), w]
    if aux is not None:
        in_specs.append(_const_spec(aux.shape))
        args.append(aux)
    out_shape, out_specs = [], []
    for (c0, c1, dtypes, _) in segs:
        for dt in dtypes:
            if dt == "T":
                out_shape.append(jax.ShapeDtypeStruct((m // rows_per_batch, c1 - c0, rows_per_batch), F32))
                out_specs.append(pl.BlockSpec((1, c1 - c0, tm), lambda i: (i // tiles_per_b, 0, i % tiles_per_b)))
            else:
                out_shape.append(jax.ShapeDtypeStruct((m, c1 - c0), dt))
                out_specs.append(pl.BlockSpec((tm, c1 - c0), lambda i: (i, 0)))
    return pl.pallas_call(
        functools.partial(_norm_proj_kernel, segs=tuple(segs), has_aux=aux is not None),
        grid=(m // tm,),
        in_specs=in_specs,
        out_specs=out_specs,
        out_shape=out_shape,
        compiler_params=_cparams(("parallel",)),
        name="norm_proj",
    )(*args)


def _mm_res_kernel(*refs, n, proj):
    a_refs, w_refs = refs[:n], refs[n:2 * n]
    res_ref = refs[2 * n]
    acc = res_ref[...]
    for a_ref, w_ref in zip(a_refs, w_refs):
        acc = acc + jnp.dot(a_ref[...].astype(BF16), w_ref[...], preferred_element_type=F32)
    if proj:
        g_ref, wp_ref, o_ref, p_ref = refs[2 * n + 1:]
        ms = jnp.mean(acc * acc, axis=-1, keepdims=True)
        xb = ((acc * lax.rsqrt(ms + RMS_EPS)) * g_ref[...]).astype(BF16)
        p_ref[...] = jnp.dot(xb, wp_ref[...], preferred_element_type=F32).astype(p_ref.dtype)
    else:
        o_ref = refs[2 * n + 1]
    o_ref[...] = acc


def matmul_residual(a_list, w_list, res, proj=None, tm=512):
    m, d = res.shape
    tm = min(tm, m)
    assert m % tm == 0
    n = len(a_list)
    in_specs = [pl.BlockSpec((tm, a.shape[1]), lambda i: (i, 0)) for a in a_list]
    in_specs += [_const_spec(w.shape) for w in w_list]
    in_specs.append(pl.BlockSpec((tm, d), lambda i: (i, 0)))
    args = [*a_list, *w_list, res]
    out_specs = pl.BlockSpec((tm, d), lambda i: (i, 0))
    out_shape = jax.ShapeDtypeStruct((m, d), F32)
    if proj is not None:
        g, wp = proj
        in_specs += [_const_spec((1, d)), _const_spec(wp.shape)]
        args += [g.reshape(1, d).astype(F32), wp]
        out_specs = [out_specs, pl.BlockSpec((tm, wp.shape[1]), lambda i: (i, 0))]
        out_shape = [out_shape, jax.ShapeDtypeStruct((m, wp.shape[1]), BF16)]
    return pl.pallas_call(
        functools.partial(_mm_res_kernel, n=n, proj=proj is not None),
        grid=(m // tm,),
        in_specs=in_specs,
        out_specs=out_specs,
        out_shape=out_shape,
        compiler_params=_cparams(("parallel",)),
        name="matmul_residual",
    )(*args)


FFN_CHUNK = 512


def _ffn_kernel(*refs, final_norm, pre):
    refs = list(refs)
    x_ref = refs.pop(0)
    x = x_ref[...]
    if pre:
        a_ref, wa_ref = refs.pop(0), refs.pop(0)
        x = x + jnp.dot(a_ref[...], wa_ref[...], preferred_element_type=F32)
    if final_norm:
        g_ref, wg_ref, wu_ref, wo_ref, gf_ref, o_ref, acc_ref = refs
    else:
        g_ref, wg_ref, wu_ref, wo_ref, o_ref, acc_ref = refs
    ms = jnp.mean(x * x, axis=-1, keepdims=True)
    xb = ((x * lax.rsqrt(ms + RMS_EPS)) * g_ref[...]).astype(BF16)
    hdim = wg_ref.shape[1]
    acc_ref[...] = x
    for c0 in range(0, hdim, FFN_CHUNK):
        c1 = min(c0 + FFN_CHUNK, hdim)
        gt = jnp.dot(xb, wg_ref[:, c0:c1], preferred_element_type=F32)
        up = jnp.dot(xb, wu_ref[:, c0:c1], preferred_element_type=F32)
        act = (_silu(gt) * up).astype(BF16)
        acc_ref[...] += jnp.dot(act, wo_ref[c0:c1, :], preferred_element_type=F32)
    y = acc_ref[...]
    if final_norm:
        ms2 = jnp.mean(y * y, axis=-1, keepdims=True)
        y = (y * lax.rsqrt(ms2 + RMS_EPS)) * gf_ref[...]
    o_ref[...] = y


def ffn(x, g, wg, wu, wo, g_final=None, pre=None, tm=512):
    m, d = x.shape
    tm = min(tm, m)
    assert m % tm == 0
    final_norm = g_final is not None
    in_specs = [pl.BlockSpec((tm, d), lambda i: (i, 0))]
    args = [x]
    if pre is not None:
        a, wa = pre
        in_specs += [pl.BlockSpec((tm, a.shape[1]), lambda i: (i, 0)), _const_spec(wa.shape)]
        args += [a, wa]
    in_specs += [_const_spec((1, d)), _const_spec(wg.shape), _const_spec(wu.shape), _const_spec(wo.shape)]
    args += [g.reshape(1, d).astype(F32), wg, wu, wo]
    if final_norm:
        in_specs.append(_const_spec((1, d)))
        args.append(g_final.reshape(1, d).astype(F32))
    return pl.pallas_call(
        functools.partial(_ffn_kernel, final_norm=final_norm, pre=pre is not None),
        grid=(m // tm,),
        in_specs=in_specs,
        out_specs=pl.BlockSpec((tm, d), lambda i: (i, 0)),
        out_shape=jax.ShapeDtypeStruct((m, d), F32),
        scratch_shapes=[pltpu.VMEM((tm, d), F32)],
        compiler_params=_cparams(("parallel",)),
        name="ffn",
    )(*args)


def _ssd_kernel(*refs, L, valid, has_h0, nb):
    for bi in range(nb):
        _ssd_row(refs, bi, L=L, valid=valid, has_h0=has_h0)


def _ssd_row(refs, bi, *, L, valid, has_h0):
    if has_h0:
        (xbc_ref, z_ref, dtr_ref, tail_ref, h0_ref, cw_ref, cb_ref, dtb_ref, al_ref, ale_ref,
         dsk_ref, gn_ref, y_ref, hout_ref, xpad_all, hT_all) = refs
    else:
        (xbc_ref, z_ref, dtr_ref, tail_ref, cw_ref, cb_ref, dtb_ref, al_ref, ale_ref,
         dsk_ref, gn_ref, y_ref, hout_ref, xpad_all, hT_all) = refs
    xpad_scr = xpad_all.at[bi]
    hT_scr = hT_all.at[bi]
    c = pl.program_id(1)
    nc = pl.num_programs(1)

    @pl.when(c == 0)
    def _():
        xpad_scr[0:8, :] = tail_ref[bi]
        if has_h0:
            hT_scr[...] = h0_ref[bi].T
        else:
            hT_scr[...] = jnp.zeros(hT_scr.shape, F32)

    @pl.when(c > 0)
    def _():
        xpad_scr[0:8, :] = xpad_scr[L:L + 8, :]

    xpad_scr[8:8 + L, :] = xbc_ref[bi]
    acc = cb_ref[...]
    for j in range(CONV_K):
        acc = acc + xpad_scr[5 + j:5 + j + L, :] * cw_ref[j:j + 1, :]
    xc = _silu(acc)
    xs = xc[:, :SSM_WIDTH]
    gn_w = SSM_GROUPS * SSM_N
    bm = xc[:, SSM_WIDTH:SSM_WIDTH + gn_w]
    cm = xc[:, SSM_WIDTH + gn_w:]

    dt = _softplus(dtr_ref[bi] + dtb_ref[...])
    if valid < L:
        dt = jnp.where(_iota((L, LANES), 0) < valid, dt, 0.0)
    a128 = -jnp.exp(al_ref[...])
    a_exp = -jnp.exp(ale_ref[...])

    expand = (_iota((LANES, SSM_WIDTH), 1) // SSM_P == _iota((LANES, SSM_WIDTH), 0)).astype(BF16)
    tril = (_iota((L, L), 0) >= _iota((L, L), 1))
    tril_b = tril.astype(BF16)

    dt_exp = _dot01_right(dt, expand)
    cs_exp = _dot01_left(tril_b, dt_exp * a_exp)
    cs = _dot01_left(tril_b, dt * a128)
    cs_t = cs.T

    last = cs_exp[L - 1:L, :]
    ecs = jnp.exp(cs_exp)
    wend = jnp.exp(last - cs_exp)
    dec = jnp.exp(last)
    xdt = xs * dt_exp
    xdt_b = xdt.astype(BF16)
    xw_b = (xdt * wend).astype(BF16)

    lane_lo = _iota((L, LANES), 1) < SSM_P
    gw = SSM_WIDTH // SSM_GROUPS
    y_parts = []
    for g in range(SSM_GROUPS):
        bg = bm[:, g * SSM_N:(g + 1) * SSM_N]
        cg_b = cm[:, g * SSM_N:(g + 1) * SSM_N].astype(BF16)
        cb = lax.dot_general(cg_b, bg.astype(BF16), NT_DIMS, preferred_element_type=F32)
        h_old = hT_scr[:, g * gw:(g + 1) * gw]
        y_state = jnp.dot(cg_b, h_old.astype(BF16), preferred_element_type=F32)
        pairs = []
        for i in range(gw // LANES):
            slab = g * (gw // LANES) + i
            xp = xdt_b[:, slab * LANES:(slab + 1) * LANES]
            halves = []
            for half in range(2):
                h = 2 * slab + half
                diff = cs[:, h:h + 1] - cs_t[h:h + 1, :]
                dm = jnp.exp(jnp.where(tril, diff, NEG_INF))
                sc = (cb * dm).astype(BF16)
                halves.append(jnp.dot(sc, xp, preferred_element_type=F32))
            pairs.append(jnp.where(lane_lo, halves[0], halves[1]))
        y_intra = jnp.concatenate(pairs, axis=1)
        y_parts.append(y_intra + y_state * ecs[:, g * gw:(g + 1) * gw])
        bg_t = bg.T.astype(BF16)
        upd = jnp.dot(bg_t, xw_b[:, g * gw:(g + 1) * gw], preferred_element_type=F32)
        hT_scr[:, g * gw:(g + 1) * gw] = h_old * dec[:, g * gw:(g + 1) * gw] + upd
    y = jnp.concatenate(y_parts, axis=1) + dsk_ref[...] * xs
    yg = y * _silu(z_ref[bi])
    ms = jnp.mean(yg * yg, axis=-1, keepdims=True)
    y_ref[bi] = ((yg * lax.rsqrt(ms + RMS_EPS)) * gn_ref[...]).astype(y_ref.dtype)

    @pl.when(c == nc - 1)
    def _():
        hout_ref[bi] = hT_scr[...].T


def ssd(xbc, z, dtr, tail, h0, conv_w, conv_b, dt_bias, a_log, d_skip, ssm_norm, L, valid, nb=1):
    b, t, _ = xbc.shape
    assert t % L == 0 and b % nb == 0
    nc = t // L
    has_h0 = h0 is not None
    pad16 = lambda v: jnp.pad(v.astype(F32), (0, LANES - v.shape[0])).reshape(1, LANES)
    rep = lambda v: jnp.repeat(v.astype(F32), SSM_P).reshape(1, SSM_WIDTH)
    row = lambda last: pl.BlockSpec((nb, L, last), lambda i, j: (i, j, 0))
    per_b = lambda s1, s2: pl.BlockSpec((nb, s1, s2), lambda i, j: (i, 0, 0))
    in_specs = [row(CONV_DIM), row(SSM_WIDTH), row(LANES), per_b(8, CONV_DIM)]
    args = [xbc, z, dtr, tail]
    if has_h0:
        in_specs.append(per_b(SSM_WIDTH, SSM_N))
        args.append(h0)
    consts = [conv_w.astype(F32), conv_b.reshape(1, CONV_DIM).astype(F32), pad16(dt_bias), pad16(a_log),
              rep(a_log), rep(d_skip), ssm_norm.reshape(1, SSM_WIDTH).astype(F32)]
    in_specs += [_const_spec(cst.shape) for cst in consts]
    args += consts
    return pl.pallas_call(
        functools.partial(_ssd_kernel, L=L, valid=valid, has_h0=has_h0, nb=nb),
        grid=(b // nb, nc),
        in_specs=in_specs,
        out_specs=[row(SSM_WIDTH), per_b(SSM_WIDTH, SSM_N)],
        out_shape=[jax.ShapeDtypeStruct((b, t, SSM_WIDTH), BF16),
                   jax.ShapeDtypeStruct((b, SSM_WIDTH, SSM_N), F32)],
        scratch_shapes=[pltpu.VMEM((nb, L + 8, CONV_DIM), F32), pltpu.VMEM((nb, SSM_N, SSM_WIDTH), F32)],
        compiler_params=_cparams(("parallel", "arbitrary")),
        name="ssd",
    )(*args)


GROUP = N_HEADS // N_KV


def _head_perm():
    order = []
    for slab in range(N_HEADS // 2):
        j, g = slab // GROUP, slab % GROUP
        order += [(2 * j) * GROUP + g, (2 * j + 1) * GROUP + g]
    return np.asarray(order)


HEAD_PERM = _head_perm()
COL_PERM = (HEAD_PERM[:, None] * HEAD_DIM + np.arange(HEAD_DIM)[None, :]).reshape(-1)
SLABS_PER_STEP = 4
HEADS_PER_STEP = 2 * SLABS_PER_STEP


LOG2E = math.log2(math.e)
Q_SCALE = LOG2E * HEAD_DIM ** -0.5


def _masked_q(q_ref, qm_scr):
    tq = q_ref.shape[1]
    lo = _iota((tq, LANES), 1) < HEAD_DIM
    for s in range(SLABS_PER_STEP):
        slab = q_ref[0, :, s * LANES:(s + 1) * LANES]
        zero = jnp.zeros_like(slab)
        qm_scr[2 * s] = jnp.where(lo, slab, zero)
        qm_scr[2 * s + 1] = jnp.where(lo, zero, slab)


def _write_heads(o_ref, acc, inv_l):
    tq = o_ref.shape[1]
    lo = _iota((tq, LANES), 1) < HEAD_DIM
    for s in range(SLABS_PER_STEP):
        a = acc[2 * s] * inv_l[2 * s]
        b = acc[2 * s + 1] * inv_l[2 * s + 1]
        o_ref[0, :, s * LANES:(s + 1) * LANES] = jnp.where(lo, a, b).astype(o_ref.dtype)


def _swa_prompt_kernel(q_ref, kp_ref, kc_ref, vp_ref, vc_ref, sink_ref, o_ref):
    i = pl.program_id(1)
    w = WINDOW
    r = _iota((w, 2 * w), 0)
    col = _iota((w, 2 * w), 1)
    lim = jnp.where(i > 0, r, 2 * w)
    mask = ((col > lim) & (col < w)) | ((col >= w) & (col - w <= r))
    lo = _iota((w, LANES), 1) < HEAD_DIM
    slab_of = lambda ref, s: ref[0, :, s * LANES:(s + 1) * LANES]
    npair = N_KV // 2
    kcat = [jnp.concatenate([slab_of(kp_ref, j), slab_of(kc_ref, j)], axis=0) for j in range(npair)]
    vcat = [jnp.concatenate([slab_of(vp_ref, j), slab_of(vc_ref, j)], axis=0) for j in range(npair)]
    heads = range(N_HEADS)
    scores = []
    for h in heads:
        qs = slab_of(q_ref, h // 2)
        zero = jnp.zeros_like(qs)
        qm = jnp.where(lo, qs, zero) if h % 2 == 0 else jnp.where(lo, zero, qs)
        s = lax.dot_general(qm, kcat[h // HEADS_PER_STEP], NT_DIMS, preferred_element_type=F32)
        scores.append(jnp.where(mask, s, NEG_INF))
    probs, inv_l = [], []
    for h in heads:
        sink = sink_ref[h:h + 1, :] * LOG2E
        m = jnp.maximum(jnp.max(scores[h], axis=-1, keepdims=True), sink)
        p = jnp.exp2(scores[h] - jnp.tile(m, (1, 2 * w // LANES)))
        inv_l.append(1.0 / (jnp.sum(p, axis=-1, keepdims=True) + jnp.exp2(sink - m)))
        probs.append(p.astype(BF16))
    acc = [jnp.dot(probs[h], vcat[h // HEADS_PER_STEP], preferred_element_type=F32) * inv_l[h] for h in heads]
    for s in range(N_HEADS // 2):
        o_ref[0, :, s * LANES:(s + 1) * LANES] = jnp.where(lo, acc[2 * s], acc[2 * s + 1]).astype(o_ref.dtype)


def swa_prompt(q, kb, vb, sink_perm):
    b, t, _ = q.shape
    w = WINDOW
    nb = t // w
    kvw = N_KV * HEAD_DIM
    sink_arr = jnp.broadcast_to(sink_perm.astype(F32).reshape(N_HEADS, 1), (N_HEADS, LANES))
    qspec = pl.BlockSpec((1, w, N_HEADS * HEAD_DIM), lambda bi, i: (bi, i, 0))
    cur = pl.BlockSpec((1, w, kvw), lambda bi, i: (bi, i, 0))
    prev = pl.BlockSpec((1, w, kvw), lambda bi, i: (bi, jnp.maximum(i - 1, 0), 0))
    return pl.pallas_call(
        _swa_prompt_kernel,
        grid=(b, nb),
        in_specs=[qspec, prev, cur, prev, cur, _const_spec((N_HEADS, LANES))],
        out_specs=qspec,
        out_shape=jax.ShapeDtypeStruct(q.shape, BF16),
        compiler_params=_cparams(("parallel", "arbitrary")),
        name="swa_prompt",
    )(q, kb, kb, vb, vb, sink_arr)


def _cumsum_t_kernel(lf_ref, o_ref, carry_scr, *, rows):
    c = pl.program_id(1)

    @pl.when(c == 0)
    def _():
        carry_scr[...] = jnp.zeros(carry_scr.shape, F32)

    L = lf_ref.shape[1]
    tril_b = (_iota((L, L), 0) >= _iota((L, L), 1)).astype(BF16)
    cs = _dot01_left(tril_b, lf_ref[0]) + carry_scr[...]
    carry_scr[...] = cs[L - 1:L, :]
    o_ref[0] = cs.T[0:rows, :] * LOG2E


def cumsum_t(lf, rows):
    b, t, _ = lf.shape
    L = min(512, t)
    return pl.pallas_call(
        functools.partial(_cumsum_t_kernel, rows=rows),
        grid=(b, t // L),
        in_specs=[pl.BlockSpec((1, L, LANES), lambda i, j: (i, j, 0))],
        out_specs=pl.BlockSpec((1, rows, L), lambda i, j: (i, 0, j)),
        out_shape=jax.ShapeDtypeStruct((b, rows, t), F32),
        scratch_shapes=[pltpu.VMEM((1, LANES), F32)],
        compiler_params=_cparams(("parallel", "arbitrary")),
        name="cumsum_t",
    )(lf)


def _fox_prompt_kernel(qi_ref, ki_ref, q_ref, k_ref, v_ref, c_ref, o_ref, qm_scr, m_scr, l_scr, acc_scr, *, tq, tk):
    step = pl.program_id(2)
    qi = qi_ref[step]
    ki = ki_ref[step]
    last_k = (qi * tq + tq - 1) // tk

    @pl.when(ki == 0)
    def _():
        _masked_q(q_ref, qm_scr)
        m_scr[...] = jnp.full(m_scr.shape, NEG_INF, F32)
        l_scr[...] = jnp.zeros(l_scr.shape, F32)
        acc_scr[...] = jnp.zeros(acc_scr.shape, F32)

    def body(masked):
        kb = k_ref[0]
        vb = v_ref[0]
        heads = range(HEADS_PER_STEP)
        if masked:
            row = qi * tq + _iota((tq, tk), 0)
            col = ki * tk + _iota((tq, tk), 1)
            keep = col <= row
        scores = []
        for h in heads:
            s = lax.dot_general(qm_scr[h], kb, NT_DIMS, preferred_element_type=F32) - c_ref[0, h:h + 1, :]
            scores.append(jnp.where(keep, s, NEG_INF) if masked else s)
        probs, alphas = [], []
        for h in heads:
            m_prev = m_scr[h]
            m_new = jnp.maximum(m_prev, jnp.max(scores[h], axis=-1, keepdims=True))
            alpha = jnp.exp2(m_prev - m_new)
            p = jnp.exp2(scores[h] - jnp.tile(m_new, (1, tk // LANES)))
            l_scr[h] = alpha * l_scr[h] + jnp.sum(p, axis=-1, keepdims=True)
            m_scr[h] = m_new
            probs.append(p.astype(BF16))
            alphas.append(alpha)
        for h in heads:
            acc_scr[h] = alphas[h] * acc_scr[h] + jnp.dot(probs[h], vb, preferred_element_type=F32)

    needs_mask = (ki + 1) * tk - 1 > qi * tq

    @pl.when(needs_mask)
    def _():
        body(True)

    @pl.when(jnp.logical_not(needs_mask))
    def _():
        body(False)

    @pl.when(ki == last_k)
    def _():
        _write_heads(o_ref, [acc_scr[h] for h in range(HEADS_PER_STEP)],
                     [1.0 / l_scr[h] for h in range(HEADS_PER_STEP)])


def fox_prompt(q, kb, vb, cum_t, tq=512, tk=512):
    b, t, _ = q.shape
    tq, tk = min(tq, t), min(tk, t)
    npair = N_KV // 2
    qi_l, ki_l = [], []
    for qi in range(t // tq):
        for ki in range((qi * tq + tq - 1) // tk + 1):
            qi_l.append(qi)
            ki_l.append(ki)
    qi_arr = jnp.asarray(qi_l, jnp.int32)
    ki_arr = jnp.asarray(ki_l, jnp.int32)
    qspec = pl.BlockSpec((1, tq, SLABS_PER_STEP * LANES), lambda bi, j, s, qi, ki: (bi, qi[s], j))
    kspec = pl.BlockSpec((1, tk, LANES), lambda bi, j, s, qi, ki: (bi, ki[s], j))
    cspec = pl.BlockSpec((1, HEADS_PER_STEP, tk), lambda bi, j, s, qi, ki: (bi, j, ki[s]))
    grid_spec = pltpu.PrefetchScalarGridSpec(
        num_scalar_prefetch=2,
        grid=(b, npair, len(qi_l)),
        in_specs=[qspec, kspec, kspec, cspec],
        out_specs=qspec,
        scratch_shapes=[pltpu.VMEM((HEADS_PER_STEP, tq, LANES), BF16),
                        pltpu.VMEM((HEADS_PER_STEP, tq, LANES), F32),
                        pltpu.VMEM((HEADS_PER_STEP, tq, LANES), F32),
                        pltpu.VMEM((HEADS_PER_STEP, tq, LANES), F32)],
    )
    return pl.pallas_call(
        functools.partial(_fox_prompt_kernel, tq=tq, tk=tk),
        grid_spec=grid_spec,
        out_shape=jax.ShapeDtypeStruct(q.shape, BF16),
        compiler_params=_cparams(("parallel", "parallel", "arbitrary")),
        name="fox_prompt",
    )(qi_arr, ki_arr, q, kb, vb, cum_t)


def _xattn_kernel(q_ref, k_ref, v_ref, o_ref, *, nb, mem, interleaved):
    scale = LOG2E * XA_DIM ** -0.5
    pairs = [(bi, h) for bi in range(nb) for h in range(XA_HEADS)]
    lanes = lambda h: slice(h * XA_DIM, (h + 1) * XA_DIM)

    def kv_head(ref, bi, h):
        if interleaved:
            return ref[bi, pl.ds(h, mem, stride=XA_HEADS), :].astype(BF16)
        return ref[bi, :, lanes(h)].astype(BF16)

    scores = [lax.dot_general(q_ref[bi, :, lanes(h)], kv_head(k_ref, bi, h), NT_DIMS,
                              preferred_element_type=F32) * scale for bi, h in pairs]
    probs, inv_l = [], []
    for s in scores:
        p = jnp.exp2(s - jnp.max(s, axis=-1, keepdims=True))
        inv_l.append(1.0 / jnp.sum(p, axis=-1, keepdims=True))
        probs.append(p.astype(BF16))
    for (bi, h), p, il in zip(pairs, probs, inv_l):
        o = jnp.dot(p, kv_head(v_ref, bi, h), preferred_element_type=F32) * il
        o_ref[bi, :, lanes(h)] = o.astype(o_ref.dtype)


def xattn(q, mk, mv, tq, nb, kv_first=0):
    b, t, _ = q.shape
    interleaved = mk.shape[2] == XA_DIM
    mem = mk.shape[1] // XA_HEADS if interleaved else mk.shape[1]
    assert kv_first % nb == 0
    off = kv_first // nb
    qspec = pl.BlockSpec((nb, tq, XA_WIDTH), lambda i, j: (i, j, 0))
    mspec = pl.BlockSpec((nb,) + mk.shape[1:], lambda i, j: (i + off, 0, 0))
    return pl.pallas_call(
        functools.partial(_xattn_kernel, nb=nb, mem=mem, interleaved=interleaved),
        grid=(b // nb, t // tq),
        in_specs=[qspec, mspec, mspec],
        out_specs=qspec,
        out_shape=jax.ShapeDtypeStruct(q.shape, BF16),
        compiler_params=_cparams(("parallel", "arbitrary")),
        name="xattn",
    )(q, mk, mv)


def _block_diag_q(q, nb, t):
    q4 = q.reshape(nb, t * N_HEADS, HEAD_DIM)
    qt = jnp.tile(q4, (1, 1, N_KV))
    row_kv = (np.arange(t * N_HEADS) % N_HEADS) // GROUP
    keep = jnp.asarray(row_kv[:, None] == (np.arange(N_KV * HEAD_DIM) // HEAD_DIM)[None, :])
    return jnp.where(keep[None], qt * jnp.asarray(HEAD_DIM ** -0.5, BF16), jnp.zeros((), BF16))


def _diag_extract(acc, rows):
    row_kv = (_iota((rows, HEAD_DIM), 0) % N_HEADS) // GROUP
    out = jnp.zeros((rows, HEAD_DIM), F32)
    for kv in range(N_KV):
        out = out + jnp.where(row_kv == kv, acc[:, kv * HEAD_DIM:(kv + 1) * HEAD_DIM], 0.0)
    return out


NEW_PAD = 16


def _swa_sample_kernel(q_ref, ck_ref, cv_ref, kn_ref, vn_ref, sink_ref, o_ref, ok_ref, ov_ref, *, nb, t):
    w = WINDOW
    rows = t * N_HEADS
    kvw = N_KV * HEAD_DIM
    tok = _iota((rows, 2 * w), 0) // N_HEADS
    col = _iota((rows, 2 * w), 1)
    mask = ((col > tok) & (col < w)) | ((col >= w) & (col - w <= tok))
    sink = sink_ref[:, 0:1]
    zpad = jnp.zeros((w - NEW_PAD, kvw), BF16)
    for bi in range(nb):
        ck = ck_ref[bi]
        cv = cv_ref[bi]
        kn = kn_ref[bi]
        vn = vn_ref[bi]
        kcat = jnp.concatenate([ck.astype(BF16), kn.astype(BF16), zpad], axis=0)
        vcat = jnp.concatenate([cv.astype(BF16), vn.astype(BF16), zpad], axis=0)
        s = lax.dot_general(q_ref[bi], kcat, NT_DIMS, preferred_element_type=F32)
        s = jnp.where(mask, s, NEG_INF)
        m = jnp.maximum(jnp.max(s, axis=-1, keepdims=True), sink)
        p = jnp.exp(s - m)
        l = jnp.sum(p, axis=-1, keepdims=True) + jnp.exp(sink - m)
        acc = jnp.dot(p.astype(BF16), vcat, preferred_element_type=F32)
        o_ref[bi] = (_diag_extract(acc, rows) * (1.0 / l)).astype(o_ref.dtype)
        ok_ref[bi, 0:w - t, :] = ck[t:w, :]
        ov_ref[bi, 0:w - t, :] = cv[t:w, :]
        ok_ref[bi, w - t:w, :] = kn[0:t, :]
        ov_ref[bi, w - t:w, :] = vn[0:t, :]


def swa_sample(qbd, ck, cv, kn, vn, sink, t, nb=8):
    b = qbd.shape[0]
    w = WINDOW
    rows = t * N_HEADS
    sink_rows = jnp.broadcast_to(jnp.tile(sink.astype(F32), t).reshape(rows, 1), (rows, LANES))
    bspec = lambda s1, s2: pl.BlockSpec((nb, s1, s2), lambda i: (i, 0, 0))
    kvw = N_KV * HEAD_DIM
    return pl.pallas_call(
        functools.partial(_swa_sample_kernel, nb=nb, t=t),
        grid=(b // nb,),
        in_specs=[bspec(rows, kvw), bspec(w, kvw), bspec(w, kvw), bspec(NEW_PAD, kvw), bspec(NEW_PAD, kvw),
                  _const_spec((rows, LANES))],
        out_specs=[bspec(rows, HEAD_DIM), bspec(w, kvw), bspec(w, kvw)],
        out_shape=[jax.ShapeDtypeStruct((b, rows, HEAD_DIM), BF16),
                   jax.ShapeDtypeStruct((b, w, kvw), F32), jax.ShapeDtypeStruct((b, w, kvw), F32)],
        compiler_params=_cparams(("parallel",)),
        name="swa_sample",
    )(qbd, ck, cv, kn, vn, sink_rows)


FOX_PAGES_PER_STEP = 64


def _fox_sample_kernel(pt_ref, q_ref, kn_ref, vn_ref, lfn_ref, kc_hbm, vc_hbm, lf_hbm, o_ref,
                       kbuf, vbuf, lbuf, sem, carry_scr, m_scr, l_scr, acc_scr,
                       *, t, n_pages, pg):
    b = pl.program_id(0)
    c = pl.program_id(1)
    nch = pl.num_programs(1)
    step = b * nch + c
    total = pl.num_programs(0) * nch
    rows = t * N_HEADS
    nh = N_HEADS

    def copies(stp, slot):
        base = (stp // nch) * n_pages + (stp % nch) * pg
        out = []
        for p in range(pg):
            page = pt_ref[base + p]
            out.append(pltpu.make_async_copy(kc_hbm.at[page], kbuf.at[slot, :, pl.ds(p * PAGE, PAGE)], sem.at[slot, 0]))
            out.append(pltpu.make_async_copy(vc_hbm.at[page], vbuf.at[slot, :, pl.ds(p * PAGE, PAGE)], sem.at[slot, 1]))
            out.append(pltpu.make_async_copy(lf_hbm.at[page], lbuf.at[slot, pl.ds(p * nh, nh)], sem.at[slot, 2]))
        return out

    slot = step % 2

    @pl.when(step == 0)
    def _():
        for cp in copies(step, slot):
            cp.start()

    @pl.when(step + 1 < total)
    def _():
        for cp in copies(step + 1, 1 - slot):
            cp.start()

    @pl.when(c == 0)
    def _():
        carry_scr[...] = jnp.zeros(carry_scr.shape, F32)
        m_scr[...] = jnp.full(m_scr.shape, NEG_INF, F32)
        l_scr[...] = jnp.zeros(l_scr.shape, F32)
        acc_scr[...] = jnp.zeros(acc_scr.shape, F32)

    for cp in copies(step, slot):
        cp.wait()

    q = q_ref[0]
    upper_b = (_iota((PAGE, PAGE), 0) <= _iota((PAGE, PAGE), 1)).astype(BF16)

    def online(s, pv):
        m_prev = m_scr[...]
        m_new = jnp.maximum(m_prev, jnp.max(s, axis=-1, keepdims=True))
        alpha = jnp.exp(m_prev - m_new)
        p = jnp.exp(s - m_new)
        l_scr[...] = alpha * l_scr[...] + jnp.sum(p, axis=-1, keepdims=True)
        acc_scr[...] = alpha * acc_scr[...] + pv(p.astype(BF16))
        m_scr[...] = m_new

    lf = lbuf[slot]
    cl = _dot01_right(lf, upper_b)
    tot = jnp.broadcast_to(cl[:, PAGE - 1:PAGE], cl.shape)
    r = _iota((pg * nh, pg * nh), 0)
    cc = _iota((pg * nh, pg * nh), 1)
    earlier_b = ((r % nh == cc % nh) & (cc // nh < r // nh)).astype(BF16)
    cum = cl + _dot01_left(earlier_b, tot) + jnp.concatenate([carry_scr[...]] * pg, axis=0)
    carry_new = jnp.broadcast_to(cum[(pg - 1) * nh:pg * nh, PAGE - 1:PAGE], (nh, LANES))
    carry_scr[...] = carry_new
    bias16 = jnp.concatenate([cum[p * nh:(p + 1) * nh, :] for p in range(pg)], axis=1)
    bias = jnp.concatenate([bias16] * t, axis=0)

    kc = kbuf[slot].astype(BF16)
    vc = vbuf[slot].astype(BF16)
    s = jnp.dot(q, kc, preferred_element_type=F32) - bias
    online(s, lambda p: lax.dot_general(p, vc, NT_DIMS, preferred_element_type=F32))

    @pl.when(c == nch - 1)
    def _():
        zpad = jnp.zeros((PAGE - NEW_PAD, N_KV * HEAD_DIM), BF16)
        knew = jnp.concatenate([kn_ref[0].astype(BF16), zpad], axis=0)
        vnew = jnp.concatenate([vn_ref[0].astype(BF16), zpad], axis=0)
        cum_n = _dot01_right(lfn_ref[0], upper_b) + carry_new
        sn = lax.dot_general(q, knew, NT_DIMS, preferred_element_type=F32)
        sn = sn - jnp.concatenate([cum_n] * t, axis=0)
        tok = _iota((rows, PAGE), 0) // nh
        sn = jnp.where(_iota((rows, PAGE), 1) <= tok, sn, NEG_INF)
        online(sn, lambda p: jnp.dot(p, vnew, preferred_element_type=F32))
        o_ref[0] = (_diag_extract(acc_scr[...], rows) * (1.0 / l_scr[...])).astype(o_ref.dtype)


def fox_sample(page_table, qbd, kn, vn, lfn_t, kcache, vcache, lf_t_cache, t, pg=FOX_PAGES_PER_STEP):
    b, n_pages = page_table.shape
    pg = min(pg, n_pages)
    assert n_pages % pg == 0
    rows = t * N_HEADS
    kvw = N_KV * HEAD_DIM
    bspec = lambda s1, s2: pl.BlockSpec((1, s1, s2), lambda i, j, pt: (i, 0, 0))
    any_spec = pl.BlockSpec(memory_space=pl.ANY)
    grid_spec = pltpu.PrefetchScalarGridSpec(
        num_scalar_prefetch=1,
        grid=(b, n_pages // pg),
        in_specs=[bspec(rows, kvw), bspec(NEW_PAD, kvw), bspec(NEW_PAD, kvw), bspec(N_HEADS, LANES),
                  any_spec, any_spec, any_spec],
        out_specs=bspec(rows, HEAD_DIM),
        scratch_shapes=[pltpu.VMEM((2, kvw, pg * PAGE), F32), pltpu.VMEM((2, kvw, pg * PAGE), F32),
                        pltpu.VMEM((2, pg * N_HEADS, LANES), F32), pltpu.SemaphoreType.DMA((2, 3)),
                        pltpu.VMEM((N_HEADS, LANES), F32),
                        pltpu.VMEM((rows, 1), F32), pltpu.VMEM((rows, 1), F32), pltpu.VMEM((rows, kvw), F32)],
    )
    return pl.pallas_call(
        functools.partial(_fox_sample_kernel, t=t, n_pages=n_pages, pg=pg),
        grid_spec=grid_spec,
        out_shape=jax.ShapeDtypeStruct((b, rows, HEAD_DIM), BF16),
        compiler_params=_cparams(("arbitrary", "arbitrary")),
        name="fox_sample",
    )(page_table.reshape(-1), qbd, kn, vn, lfn_t, kcache, vcache, lf_t_cache)


def _pad_cols(w, width):
    return jnp.pad(w, ((0, 0), (0, width - w.shape[1])))


def kernel(x_prompt, x_sample, state_ssm, state_conv, cache_swa_k, cache_swa_v, cache_fox_k, cache_fox_v,
           cache_fox_logf, cache_mem_k, cache_mem_v, page_table, mem_prompt, norm_mix, norm_xa, norm_mem,
           norm_ffn, w_in_even, conv_w, conv_b, dt_bias, a_log, d_skip, ssm_norm, swa_sink, w_out_even,
           w_in_odd, fox_fb, w_out_odd, w_xq, w_xk, w_xv, w_xo, w_ffn_in, w_ffn_out, norm_final):
    bp, seq, d = x_prompt.shape
    bs, ts, _ = x_sample.shape
    depth = norm_mix.shape[0]
    mem_len = mem_prompt.shape[1]
    kvw = N_KV * HEAD_DIM
    qw = N_HEADS * HEAD_DIM
    xp = x_prompt.reshape(bp * seq, d)
    xs = x_sample.reshape(bs * ts, d)
    mem2 = mem_prompt.reshape(bp * mem_len, d)
    outs = {k: [] for k in ("p_ssm", "p_conv", "p_swk", "p_swv", "p_fk", "p_fv", "p_fl", "p_mk", "p_mv",
                            "s_ssm", "s_conv", "s_swk", "s_swv", "s_fk", "s_fv", "s_fl")}

    for l in range(depth):
        li = l // 2
        if l % 2 == 0:
            w = w_in_even[li]
            o_z, o_xbc, o_dt = 0, SSM_WIDTH, SSM_WIDTH + CONV_DIM
            o_q = o_dt + SSM_HEADS
            o_k, o_v = o_q + qw, o_q + qw + kvw
            wz, wxbc, wdt = w[:, o_z:o_xbc], w[:, o_xbc:o_dt], _pad_cols(w[:, o_dt:o_q], LANES)
            wq, wk, wv = w[:, o_q:o_k], w[:, o_k:o_v], w[:, o_v:o_v + kvw]
            c_xbc, c_q, c_k, c_v, c_dt = SSM_WIDTH, SSM_WIDTH + CONV_DIM, SSM_WIDTH + CONV_DIM + qw, \
                SSM_WIDTH + CONV_DIM + qw + kvw, SSM_WIDTH + CONV_DIM + qw + 2 * kvw
            n_tot = c_dt + LANES
            wo = w_out_even[li]
            wo_ssm = wo[:SSM_WIDTH].astype(BF16)
            wo_att = wo[SSM_WIDTH:]
            ssm_params = (conv_w[li], conv_b[li], dt_bias[li], a_log[li], d_skip[li], ssm_norm[li])

            w_p = jnp.concatenate([wz, wxbc, wq[:, COL_PERM], wk, wv, wdt], axis=1).astype(BF16)
            segs_p = [(0, c_xbc, (F32,), None), (c_xbc, c_q, (F32,), None), (c_q, c_k, (BF16,), Q_SCALE),
                      (c_k, c_v, (F32, BF16), None), (c_v, c_dt, (F32, BF16), None), (c_dt, n_tot, (F32,), None)]
            z, xbc, q, k, kb, v, vb, dtr = norm_proj(xp, norm_mix[l], w_p, segs_p)
            xbc3 = xbc.reshape(bp, seq, CONV_DIM)
            y_ssm, h = ssd(xbc3, z.reshape(bp, seq, SSM_WIDTH), dtr.reshape(bp, seq, LANES),
                           jnp.zeros((bp, 8, CONV_DIM), F32), None, *ssm_params, L=SSM_CHUNK, valid=SSM_CHUNK,
                           nb=math.gcd(bp, 4))
            y_att = swa_prompt(q.reshape(bp, seq, qw), kb.reshape(bp, seq, kvw), vb.reshape(bp, seq, kvw),
                               swa_sink[li][HEAD_PERM])
            mix_p = ([y_ssm.reshape(bp * seq, SSM_WIDTH), y_att.reshape(bp * seq, qw)],
                     [wo_ssm, wo_att[COL_PERM].astype(BF16)])
            last_w = lambda a: a.reshape(bp, seq, kvw)[:, seq - WINDOW:].reshape(bp, WINDOW, N_KV, HEAD_DIM)
            outs["p_ssm"].append(h.reshape(bp, SSM_HEADS, SSM_P, SSM_N))
            outs["p_conv"].append(xbc3[:, seq - (CONV_K - 1):])
            outs["p_swk"].append(last_w(k))
            outs["p_swv"].append(last_w(v))

            w_s = jnp.concatenate([wz, wxbc, wq, wk, wv, wdt], axis=1).astype(BF16)
            segs_s = [(0, c_xbc, (F32,), None), (c_xbc, c_q, (F32,), None), (c_q, c_k, (BF16,), None),
                      (c_k, c_v, (F32,), None), (c_v, c_dt, (F32,), None), (c_dt, n_tot, (F32,), None)]
            z, xbc, q, k, v, dtr = norm_proj(xs, norm_mix[l], w_s, segs_s)
            ls = NEW_PAD
            pad_t = lambda a: jnp.pad(a.reshape(bs, ts, a.shape[-1]), ((0, 0), (0, ls - ts), (0, 0)))
            xbc3 = xbc.reshape(bs, ts, CONV_DIM)
            tail = jnp.pad(state_conv[li].astype(F32), ((0, 0), (8 - (CONV_K - 1), 0), (0, 0)))
            y_ssm, h = ssd(pad_t(xbc), pad_t(z), pad_t(dtr), tail, state_ssm[li].reshape(bs, SSM_WIDTH, SSM_N),
                           *ssm_params, L=ls, valid=ts, nb=math.gcd(bs, 8))
            y_ssm = y_ssm[:, :ts].reshape(bs * ts, SSM_WIDTH)
            kn, vn = pad_t(k), pad_t(v)
            o_att, new_k, new_v = swa_sample(_block_diag_q(q, bs, ts), cache_swa_k[li].reshape(bs, WINDOW, kvw),
                                             cache_swa_v[li].reshape(bs, WINDOW, kvw), kn, vn, swa_sink[li], ts)
            mix_s = ([y_ssm, o_att.reshape(bs * ts, qw)], [wo_ssm, wo_att.astype(BF16)])
            outs["s_ssm"].append(h.reshape(bs, SSM_HEADS, SSM_P, SSM_N))
            outs["s_conv"].append(jnp.concatenate([state_conv[li].astype(F32), xbc3], axis=1)[:, -(CONV_K - 1):])
            outs["s_swk"].append(new_k.reshape(bs, WINDOW, N_KV, HEAD_DIM))
            outs["s_swv"].append(new_v.reshape(bs, WINDOW, N_KV, HEAD_DIM))
        else:
            w = w_in_odd[li]
            wq, wk, wv, wf = w[:, :qw], w[:, qw:qw + kvw], w[:, qw + kvw:qw + 2 * kvw], w[:, qw + 2 * kvw:]
            c_k, c_v, c_f = qw, qw + kvw, qw + 2 * kvw
            n_tot = c_f + LANES
            fb = fox_fb[li].astype(F32)

            wf_p = _pad_cols(jnp.concatenate([wf[:, HEAD_PERM], wf], axis=1), LANES)
            fb_p = jnp.pad(jnp.concatenate([fb[HEAD_PERM], fb]), (0, LANES - 2 * N_HEADS)).reshape(1, LANES)
            w_p = jnp.concatenate([wq[:, COL_PERM], wk, wv, wf_p], axis=1).astype(BF16)
            segs_p = [(0, c_k, (BF16,), Q_SCALE), (c_k, c_v, ("T", BF16), None), (c_v, c_f, ("T", BF16), None),
                      (c_f, n_tot, (F32,), "logsig")]
            q, k_t, kb, v_t, vb, lf = norm_proj(xp, norm_mix[l], w_p, segs_p, aux=fb_p, rows_per_batch=seq)
            lf3 = lf.reshape(bp, seq, LANES)
            cum_t = cumsum_t(lf3, N_HEADS)
            y = fox_prompt(q.reshape(bp, seq, qw), kb.reshape(bp, seq, kvw), vb.reshape(bp, seq, kvw), cum_t)
            mix_p = ([y.reshape(bp * seq, qw)], [w_out_odd[li][COL_PERM].astype(BF16)])
            untr = lambda a: jnp.transpose(a.reshape(bp, N_KV, HEAD_DIM, seq), (0, 3, 1, 2))
            outs["p_fk"].append(untr(k_t))
            outs["p_fv"].append(untr(v_t))
            outs["p_fl"].append(lf3[:, :, N_HEADS:2 * N_HEADS])

            w_s = jnp.concatenate([wq, wk, wv, _pad_cols(wf, LANES)], axis=1).astype(BF16)
            fb_s = jnp.pad(fb, (0, LANES - N_HEADS)).reshape(1, LANES)
            segs_s = [(0, c_k, (BF16,), None), (c_k, c_v, (F32,), None), (c_v, c_f, (F32,), None),
                      (c_f, n_tot, (F32,), "logsig")]
            q, k, v, lf = norm_proj(xs, norm_mix[l], w_s, segs_s, aux=fb_s)
            lf_new = lf[:, :N_HEADS].reshape(bs, ts, N_HEADS)
            pad_t = lambda a: jnp.pad(a.reshape(bs, ts, a.shape[-1]), ((0, 0), (0, NEW_PAD - ts), (0, 0)))
            lfn_t = jnp.pad(jnp.swapaxes(lf_new, 1, 2), ((0, 0), (0, 0), (0, LANES - ts)))
            pool = cache_fox_k.shape[1]
            page_t = lambda c: jnp.transpose(c[li].astype(F32), (0, 2, 3, 1)).reshape(pool, kvw, PAGE)
            y = fox_sample(page_table, _block_diag_q(q, bs, ts), pad_t(k), pad_t(v), lfn_t,
                           page_t(cache_fox_k), page_t(cache_fox_v),
                           jnp.swapaxes(cache_fox_logf[li].astype(F32), 1, 2), ts)
            mix_s = ([y.reshape(bs * ts, qw)], [w_out_odd[li].astype(BF16)])
            outs["s_fk"].append(k.reshape(bs, ts, N_KV, HEAD_DIM))
            outs["s_fv"].append(v.reshape(bs, ts, N_KV, HEAD_DIM))
            outs["s_fl"].append(lf_new)

        wkv = jnp.concatenate([w_xk[l], w_xv[l]], axis=1).astype(BF16)
        mk, mv = norm_proj(mem2, norm_mem[l], wkv, [(0, XA_WIDTH, (F32,), None), (XA_WIDTH, 2 * XA_WIDTH, (F32,), None)])
        wq_b = w_xq[l].astype(BF16)
        wo_b = w_xo[l].astype(BF16)
        xp, qx = matmul_residual(*mix_p, xp, proj=(norm_xa[l], wq_b))
        ox_p = xattn(qx.reshape(bp, seq, XA_WIDTH), mk.reshape(bp, mem_len, XA_WIDTH),
                     mv.reshape(bp, mem_len, XA_WIDTH), tq=512, nb=1).reshape(bp * seq, XA_WIDTH)
        xs, qx = matmul_residual(*mix_s, xs, proj=(norm_xa[l], wq_b))
        qx = jnp.pad(qx.reshape(bs, ts, XA_WIDTH), ((0, 0), (0, NEW_PAD - ts), (0, 0)))
        all_layers = lambda c: c.astype(F32).reshape(depth * bs, mem_len * XA_HEADS, XA_DIM)
        ox_s = xattn(qx, all_layers(cache_mem_k), all_layers(cache_mem_v), tq=NEW_PAD, nb=4, kv_first=l * bs)
        ox_s = ox_s[:, :ts].reshape(bs * ts, XA_WIDTH)
        outs["p_mk"].append(mk.reshape(bp, mem_len, XA_HEADS, XA_DIM))
        outs["p_mv"].append(mv.reshape(bp, mem_len, XA_HEADS, XA_DIM))

        wg = w_ffn_in[l][:, :FFN_H].astype(BF16)
        wu = w_ffn_in[l][:, FFN_H:].astype(BF16)
        wd = w_ffn_out[l].astype(BF16)
        gf = norm_final if l == depth - 1 else None
        xp = ffn(xp, norm_ffn[l], wg, wu, wd, g_final=gf, pre=(ox_p, wo_b))
        xs = ffn(xs, norm_ffn[l], wg, wu, wd, g_final=gf, pre=(ox_s, wo_b))

    st = lambda key: jnp.stack(outs[key])
    return (xp.reshape(bp, seq, d), xs.reshape(bs, ts, d),
            st("p_ssm"), st("p_conv"), st("p_swk"), st("p_swv"), st("p_fk"), st("p_fv"), st("p_fl"),
            st("p_mk"), st("p_mv"), st("s_ssm"), st("s_conv"), st("s_swk"), st("s_swv"),
            st("s_fk"), st("s_fv"), st("s_fl"))
```

```python
import functools
import math

import numpy as np
import jax
import jax.numpy as jnp
from jax import lax
from jax.experimental import pallas as pl
from jax.experimental.pallas import tpu as pltpu

F32 = jnp.float32
BF16 = jnp.bfloat16

D_MODEL = 1024
HEAD_DIM = 64
N_HEADS = 16
N_KV = 4
SSM_HEADS = 16
SSM_P = 64
SSM_WIDTH = 1024
SSM_N = 128
SSM_GROUPS = 2
SSM_CHUNK = 128
CONV_K = 4
CONV_DIM = 1536
WINDOW = 128
PAGE = 128
XA_HEADS = 4
XA_DIM = 128
XA_WIDTH = 512
FFN_H = 2816
RMS_EPS = 1e-6
NEG_INF = -1e30

V7X_VMEM_BYTES = 64 * 1024 * 1024
VMEM_LIMIT = V7X_VMEM_BYTES - 8 * 1024 * 1024
LANES = 128

NT_DIMS = (((1,), (1,)), ((), ()))


def _cparams(sem):
    return pltpu.CompilerParams(dimension_semantics=sem, vmem_limit_bytes=VMEM_LIMIT)


def _const_spec(shape):
    nd = len(shape)
    return pl.BlockSpec(shape, lambda *_: (0,) * nd, pipeline_mode=pl.Buffered(1))


def _softplus(x):
    return jnp.maximum(x, 0.0) + jnp.log1p(jnp.exp(-jnp.abs(x)))


def _silu(x):
    return x * (1.0 / (1.0 + jnp.exp(-x)))


def _split3(x):
    hi = x.astype(BF16)
    r1 = x - hi.astype(F32)
    mid = r1.astype(BF16)
    lo = (r1 - mid.astype(F32)).astype(BF16)
    return hi, mid, lo


def _dot01_left(m01, x):
    hi, mid, lo = _split3(x)
    d = lambda b: jnp.dot(m01, b, preferred_element_type=F32)
    return (d(lo) + d(mid)) + d(hi)


def _dot01_right(x, m01):
    hi, mid, lo = _split3(x)
    d = lambda a: jnp.dot(a, m01, preferred_element_type=F32)
    return (d(lo) + d(mid)) + d(hi)


def _iota(shape, dim):
    return lax.broadcasted_iota(jnp.int32, shape, dim)


def _norm_proj_kernel(*refs, segs, has_aux):
    x_ref, g_ref, w_ref = refs[:3]
    pos = 3
    aux_ref = None
    if has_aux:
        aux_ref = refs[3]
        pos = 4
    out_refs = refs[pos:]
    x = x_ref[...]
    ms = jnp.mean(x * x, axis=-1, keepdims=True)
    xn = (x * lax.rsqrt(ms + RMS_EPS)) * g_ref[...]
    xb = xn.astype(BF16)
    oi = 0
    for (c0, c1, dtypes, act) in segs:
        y = jnp.dot(xb, w_ref[:, c0:c1], preferred_element_type=F32)
        if act == "logsig":
            y = -_softplus(-(y + aux_ref[...]))
        elif act is not None:
            y = y * act
        for dt in dtypes:
            if dt == "T":
                out_refs[oi][0] = y.T
            else:
                out_refs[oi][...] = y.astype(dt)
            oi += 1


def norm_proj(x, g, w, segs, aux=None, tm=512, rows_per_batch=None):
    m, d = x.shape
    tm = min(tm, m)
    assert m % tm == 0
    n = w.shape[1]
    tiles_per_b = (rows_per_batch or m) // tm
    in_specs = [pl.BlockSpec((tm, d), lambda i: (i, 0)), _const_spec((1, d)), _const_spec((d, n))]
    args = [x, g.reshape(1, d).astype(F32), w]
    if aux is not None:
        in_specs.append(_const_spec(aux.shape))
        args.append(aux)
    out_shape, out_specs = [], []
    for (c0, c1, dtypes, _) in segs:
        for dt in dtypes:
            if dt == "T":
                out_shape.append(jax.ShapeDtypeStruct((m // rows_per_batch, c1 - c0, rows_per_batch), F32))
                out_specs.append(pl.BlockSpec((1, c1 - c0, tm), lambda i: (i // tiles_per_b, 0, i % tiles_per_b)))
            else:
                out_shape.append(jax.ShapeDtypeStruct((m, c1 - c0), dt))
                out_specs.append(pl.BlockSpec((tm, c1 - c0), lambda i: (i, 0)))
    return pl.pallas_call(
        functools.partial(_norm_proj_kernel, segs=tuple(segs), has_aux=aux is not None),
        grid=(m // tm,),
        in_specs=in_specs,
        out_specs=out_specs,
        out_shape=out_shape,
        compiler_params=_cparams(("parallel",)),
        name="norm_proj",
    )(*args)


def _mm_res_kernel(*refs, n, proj):
    a_refs, w_refs = refs[:n], refs[n:2 * n]
    res_ref = refs[2 * n]
    acc = res_ref[...]
    for a_ref, w_ref in zip(a_refs, w_refs):
        acc = acc + jnp.dot(a_ref[...].astype(BF16), w_ref[...], preferred_element_type=F32)
    if proj:
        g_ref, wp_ref, o_ref, p_ref = refs[2 * n + 1:]
        ms = jnp.mean(acc * acc, axis=-1, keepdims=True)
        xb = ((acc * lax.rsqrt(ms + RMS_EPS)) * g_ref[...]).astype(BF16)
        p_ref[...] = jnp.dot(xb, wp_ref[...], preferred_element_type=F32).astype(p_ref.dtype)
    else:
        o_ref = refs[2 * n + 1]
    o_ref[...] = acc


def matmul_residual(a_list, w_list, res, proj=None, tm=512):
    m, d = res.shape
    tm = min(tm, m)
    assert m % tm == 0
    n = len(a_list)
    in_specs = [pl.BlockSpec((tm, a.shape[1]), lambda i: (i, 0)) for a in a_list]
    in_specs += [_const_spec(w.shape) for w in w_list]
    in_specs.append(pl.BlockSpec((tm, d), lambda i: (i, 0)))
    args = [*a_list, *w_list, res]
    out_specs = pl.BlockSpec((tm, d), lambda i: (i, 0))
    out_shape = jax.ShapeDtypeStruct((m, d), F32)
    if proj is not None:
        g, wp = proj
        in_specs += [_const_spec((1, d)), _const_spec(wp.shape)]
        args += [g.reshape(1, d).astype(F32), wp]
        out_specs = [out_specs, pl.BlockSpec((tm, wp.shape[1]), lambda i: (i, 0))]
        out_shape = [out_shape, jax.ShapeDtypeStruct((m, wp.shape[1]), BF16)]
    return pl.pallas_call(
        functools.partial(_mm_res_kernel, n=n, proj=proj is not None),
        grid=(m // tm,),
        in_specs=in_specs,
        out_specs=out_specs,
        out_shape=out_shape,
        compiler_params=_cparams(("parallel",)),
        name="matmul_residual",
    )(*args)


FFN_CHUNK = 512


def _ffn_kernel(*refs, final_norm, pre):
    refs = list(refs)
    x_ref = refs.pop(0)
    x = x_ref[...]
    if pre:
        a_ref, wa_ref = refs.pop(0), refs.pop(0)
        x = x + jnp.dot(a_ref[...], wa_ref[...], preferred_element_type=F32)
    if final_norm:
        g_ref, wi_ref, wo_ref, gf_ref, o_ref, acc_ref = refs
    else:
        g_ref, wi_ref, wo_ref, o_ref, acc_ref = refs
    ms = jnp.mean(x * x, axis=-1, keepdims=True)
    xb = ((x * lax.rsqrt(ms + RMS_EPS)) * g_ref[...]).astype(BF16)
    hdim = wo_ref.shape[0]
    acc_ref[...] = x
    for c0 in range(0, hdim, FFN_CHUNK):
        c1 = min(c0 + FFN_CHUNK, hdim)
        gt = jnp.dot(xb, wi_ref[:, c0:c1], preferred_element_type=F32)
        up = jnp.dot(xb, wi_ref[:, hdim + c0:hdim + c1], preferred_element_type=F32)
        act = (_silu(gt) * up).astype(BF16)
        acc_ref[...] += jnp.dot(act, wo_ref[c0:c1, :], preferred_element_type=F32)
    y = acc_ref[...]
    if final_norm:
        ms2 = jnp.mean(y * y, axis=-1, keepdims=True)
        y = (y * lax.rsqrt(ms2 + RMS_EPS)) * gf_ref[...]
    o_ref[...] = y


def ffn(x, g, w_in, w_out, layer, g_final=None, pre=None, tm=512):
    m, d = x.shape
    tm = min(tm, m)
    assert m % tm == 0
    final_norm = g_final is not None
    in_specs = [pl.BlockSpec((tm, d), lambda i: (i, 0))]
    args = [x]
    if pre is not None:
        a, wa = pre
        in_specs += [pl.BlockSpec((tm, a.shape[1]), lambda i: (i, 0)), _const_spec(wa.shape)]
        args += [a, wa]
    layer_spec = lambda w: pl.BlockSpec((None,) + w.shape[1:], lambda i: (layer, 0, 0), pipeline_mode=pl.Buffered(1))
    in_specs += [_const_spec((1, d)), layer_spec(w_in), layer_spec(w_out)]
    args += [g.reshape(1, d).astype(F32), w_in, w_out]
    if final_norm:
        in_specs.append(_const_spec((1, d)))
        args.append(g_final.reshape(1, d).astype(F32))
    return pl.pallas_call(
        functools.partial(_ffn_kernel, final_norm=final_norm, pre=pre is not None),
        grid=(m // tm,),
        in_specs=in_specs,
        out_specs=pl.BlockSpec((tm, d), lambda i: (i, 0)),
        out_shape=jax.ShapeDtypeStruct((m, d), F32),
        scratch_shapes=[pltpu.VMEM((tm, d), F32)],
        compiler_params=_cparams(("parallel",)),
        name="ffn",
    )(*args)


def _ssd_kernel(*refs, L, valid, has_h0, nb):
    for bi in range(nb):
        _ssd_row(refs, bi, L=L, valid=valid, has_h0=has_h0)


def _ssd_row(refs, bi, *, L, valid, has_h0):
    if has_h0:
        (xbc_ref, z_ref, dtr_ref, tail_ref, h0_ref, cw_ref, cb_ref, dtb_ref, al_ref, ale_ref,
         dsk_ref, gn_ref, y_ref, hout_ref, xpad_all, hT_all) = refs
    else:
        (xbc_ref, z_ref, dtr_ref, tail_ref, cw_ref, cb_ref, dtb_ref, al_ref, ale_ref,
         dsk_ref, gn_ref, y_ref, hout_ref, xpad_all, hT_all) = refs
    xpad_scr = xpad_all.at[bi]
    hT_scr = hT_all.at[bi]
    c = pl.program_id(1)
    nc = pl.num_programs(1)

    @pl.when(c == 0)
    def _():
        xpad_scr[0:8, :] = tail_ref[bi]
        if has_h0:
            hT_scr[...] = h0_ref[bi].T
        else:
            hT_scr[...] = jnp.zeros(hT_scr.shape, F32)

    @pl.when(c > 0)
    def _():
        xpad_scr[0:8, :] = xpad_scr[L:L + 8, :]

    xpad_scr[8:8 + L, :] = xbc_ref[bi]
    acc = cb_ref[...]
    for j in range(CONV_K):
        acc = acc + xpad_scr[5 + j:5 + j + L, :] * cw_ref[j:j + 1, :]
    xc = _silu(acc)
    xs = xc[:, :SSM_WIDTH]
    gn_w = SSM_GROUPS * SSM_N
    bm = xc[:, SSM_WIDTH:SSM_WIDTH + gn_w]
    cm = xc[:, SSM_WIDTH + gn_w:]

    dt = _softplus(dtr_ref[bi] + dtb_ref[...])
    if valid < L:
        dt = jnp.where(_iota((L, LANES), 0) < valid, dt, 0.0)
    a128 = -jnp.exp(al_ref[...])
    a_exp = -jnp.exp(ale_ref[...])

    expand = (_iota((LANES, SSM_WIDTH), 1) // SSM_P == _iota((LANES, SSM_WIDTH), 0)).astype(BF16)
    tril = (_iota((L, L), 0) >= _iota((L, L), 1))
    tril_b = tril.astype(BF16)

    dt_exp = _dot01_right(dt, expand)
    cs_exp = _dot01_left(tril_b, dt_exp * a_exp)
    cs = _dot01_left(tril_b, dt * a128)
    cs_t = cs.T

    last = cs_exp[L - 1:L, :]
    ecs = jnp.exp(cs_exp)
    wend = jnp.exp(last - cs_exp)
    dec = jnp.exp(last)
    xdt = xs * dt_exp
    xdt_b = xdt.astype(BF16)
    xw_b = (xdt * wend).astype(BF16)

    lane_lo = _iota((L, LANES), 1) < SSM_P
    gw = SSM_WIDTH // SSM_GROUPS
    y_parts = []
    for g in range(SSM_GROUPS):
        bg = bm[:, g * SSM_N:(g + 1) * SSM_N]
        cg_b = cm[:, g * SSM_N:(g + 1) * SSM_N].astype(BF16)
        cb = lax.dot_general(cg_b, bg.astype(BF16), NT_DIMS, preferred_element_type=F32)
        h_old = hT_scr[:, g * gw:(g + 1) * gw]
        y_state = jnp.dot(cg_b, h_old.astype(BF16), preferred_element_type=F32)
        pairs = []
        for i in range(gw // LANES):
            slab = g * (gw // LANES) + i
            xp = xdt_b[:, slab * LANES:(slab + 1) * LANES]
            halves = []
            for half in range(2):
                h = 2 * slab + half
                diff = cs[:, h:h + 1] - cs_t[h:h + 1, :]
                dm = jnp.exp(jnp.where(tril, diff, NEG_INF))
                sc = (cb * dm).astype(BF16)
                halves.append(jnp.dot(sc, xp, preferred_element_type=F32))
            pairs.append(jnp.where(lane_lo, halves[0], halves[1]))
        y_intra = jnp.concatenate(pairs, axis=1)
        y_parts.append(y_intra + y_state * ecs[:, g * gw:(g + 1) * gw])
        bg_t = bg.T.astype(BF16)
        upd = jnp.dot(bg_t, xw_b[:, g * gw:(g + 1) * gw], preferred_element_type=F32)
        hT_scr[:, g * gw:(g + 1) * gw] = h_old * dec[:, g * gw:(g + 1) * gw] + upd
    y = jnp.concatenate(y_parts, axis=1) + dsk_ref[...] * xs
    yg = y * _silu(z_ref[bi])
    ms = jnp.mean(yg * yg, axis=-1, keepdims=True)
    y_ref[bi] = ((yg * lax.rsqrt(ms + RMS_EPS)) * gn_ref[...]).astype(y_ref.dtype)

    @pl.when(c == nc - 1)
    def _():
        hout_ref[bi] = hT_scr[...].T


def ssd(xbc, z, dtr, tail, h0, conv_w, conv_b, dt_bias, a_log, d_skip, ssm_norm, L, valid, nb=1):
    b, t, _ = xbc.shape
    assert t % L == 0 and b % nb == 0
    nc = t // L
    has_h0 = h0 is not None
    pad16 = lambda v: jnp.pad(v.astype(F32), (0, LANES - v.shape[0])).reshape(1, LANES)
    rep = lambda v: jnp.repeat(v.astype(F32), SSM_P).reshape(1, SSM_WIDTH)
    row = lambda last: pl.BlockSpec((nb, L, last), lambda i, j: (i, j, 0))
    per_b = lambda s1, s2: pl.BlockSpec((nb, s1, s2), lambda i, j: (i, 0, 0))
    in_specs = [row(CONV_DIM), row(SSM_WIDTH), row(LANES), per_b(8, CONV_DIM)]
    args = [xbc, z, dtr, tail]
    if has_h0:
        in_specs.append(per_b(SSM_WIDTH, SSM_N))
        args.append(h0)
    consts = [conv_w.astype(F32), conv_b.reshape(1, CONV_DIM).astype(F32), pad16(dt_bias), pad16(a_log),
              rep(a_log), rep(d_skip), ssm_norm.reshape(1, SSM_WIDTH).astype(F32)]
    in_specs += [_const_spec(cst.shape) for cst in consts]
    args += consts
    return pl.pallas_call(
        functools.partial(_ssd_kernel, L=L, valid=valid, has_h0=has_h0, nb=nb),
        grid=(b // nb, nc),
        in_specs=in_specs,
        out_specs=[row(SSM_WIDTH), per_b(SSM_WIDTH, SSM_N)],
        out_shape=[jax.ShapeDtypeStruct((b, t, SSM_WIDTH), BF16),
                   jax.ShapeDtypeStruct((b, SSM_WIDTH, SSM_N), F32)],
        scratch_shapes=[pltpu.VMEM((nb, L + 8, CONV_DIM), F32), pltpu.VMEM((nb, SSM_N, SSM_WIDTH), F32)],
        compiler_params=_cparams(("parallel", "arbitrary")),
        name="ssd",
    )(*args)


GROUP = N_HEADS // N_KV


def _head_perm():
    order = []
    for slab in range(N_HEADS // 2):
        j, g = slab // GROUP, slab % GROUP
        order += [(2 * j) * GROUP + g, (2 * j + 1) * GROUP + g]
    return np.asarray(order)


HEAD_PERM = _head_perm()
COL_PERM = (HEAD_PERM[:, None] * HEAD_DIM + np.arange(HEAD_DIM)[None, :]).reshape(-1)
SLABS_PER_STEP = 4
HEADS_PER_STEP = 2 * SLABS_PER_STEP


LOG2E = math.log2(math.e)
Q_SCALE = LOG2E * HEAD_DIM ** -0.5


def _masked_q(q_ref, qm_scr):
    tq = q_ref.shape[1]
    lo = _iota((tq, LANES), 1) < HEAD_DIM
    for s in range(SLABS_PER_STEP):
        slab = q_ref[0, :, s * LANES:(s + 1) * LANES]
        zero = jnp.zeros_like(slab)
        qm_scr[2 * s] = jnp.where(lo, slab, zero)
        qm_scr[2 * s + 1] = jnp.where(lo, zero, slab)


def _write_heads(o_ref, acc, inv_l):
    tq = o_ref.shape[1]
    lo = _iota((tq, LANES), 1) < HEAD_DIM
    for s in range(SLABS_PER_STEP):
        a = acc[2 * s] * inv_l[2 * s]
        b = acc[2 * s + 1] * inv_l[2 * s + 1]
        o_ref[0, :, s * LANES:(s + 1) * LANES] = jnp.where(lo, a, b).astype(o_ref.dtype)


def _swa_prompt_kernel(q_ref, kp_ref, kc_ref, vp_ref, vc_ref, sink_ref, o_ref):
    i = pl.program_id(1)
    w = WINDOW
    r = _iota((w, 2 * w), 0)
    col = _iota((w, 2 * w), 1)
    lim = jnp.where(i > 0, r, 2 * w)
    mask = ((col > lim) & (col < w)) | ((col >= w) & (col - w <= r))
    lo = _iota((w, LANES), 1) < HEAD_DIM
    slab_of = lambda ref, s: ref[0, :, s * LANES:(s + 1) * LANES]
    npair = N_KV // 2
    kcat = [jnp.concatenate([slab_of(kp_ref, j), slab_of(kc_ref, j)], axis=0) for j in range(npair)]
    vcat = [jnp.concatenate([slab_of(vp_ref, j), slab_of(vc_ref, j)], axis=0) for j in range(npair)]
    heads = range(N_HEADS)
    scores = []
    for h in heads:
        qs = slab_of(q_ref, h // 2)
        zero = jnp.zeros_like(qs)
        qm = jnp.where(lo, qs, zero) if h % 2 == 0 else jnp.where(lo, zero, qs)
        s = lax.dot_general(qm, kcat[h // HEADS_PER_STEP], NT_DIMS, preferred_element_type=F32)
        scores.append(jnp.where(mask, s, NEG_INF))
    probs, inv_l = [], []
    for h in heads:
        sink = sink_ref[h:h + 1, :] * LOG2E
        m = jnp.maximum(jnp.max(scores[h], axis=-1, keepdims=True), sink)
        p = jnp.exp2(scores[h] - jnp.tile(m, (1, 2 * w // LANES)))
        inv_l.append(1.0 / (jnp.sum(p, axis=-1, keepdims=True) + jnp.exp2(sink - m)))
        probs.append(p.astype(BF16))
    acc = [jnp.dot(probs[h], vcat[h // HEADS_PER_STEP], preferred_element_type=F32) * inv_l[h] for h in heads]
    for s in range(N_HEADS // 2):
        o_ref[0, :, s * LANES:(s + 1) * LANES] = jnp.where(lo, acc[2 * s], acc[2 * s + 1]).astype(o_ref.dtype)


def swa_prompt(q, kb, vb, sink_perm):
    b, t, _ = q.shape
    w = WINDOW
    nb = t // w
    kvw = N_KV * HEAD_DIM
    sink_arr = jnp.broadcast_to(sink_perm.astype(F32).reshape(N_HEADS, 1), (N_HEADS, LANES))
    qspec = pl.BlockSpec((1, w, N_HEADS * HEAD_DIM), lambda bi, i: (bi, i, 0))
    cur = pl.BlockSpec((1, w, kvw), lambda bi, i: (bi, i, 0))
    prev = pl.BlockSpec((1, w, kvw), lambda bi, i: (bi, jnp.maximum(i - 1, 0), 0))
    return pl.pallas_call(
        _swa_prompt_kernel,
        grid=(b, nb),
        in_specs=[qspec, prev, cur, prev, cur, _const_spec((N_HEADS, LANES))],
        out_specs=qspec,
        out_shape=jax.ShapeDtypeStruct(q.shape, BF16),
        compiler_params=_cparams(("parallel", "arbitrary")),
        name="swa_prompt",
    )(q, kb, kb, vb, vb, sink_arr)


def _cumsum_t_kernel(lf_ref, o_ref, carry_scr, *, rows):
    c = pl.program_id(1)

    @pl.when(c == 0)
    def _():
        carry_scr[...] = jnp.zeros(carry_scr.shape, F32)

    L = lf_ref.shape[1]
    tril_b = (_iota((L, L), 0) >= _iota((L, L), 1)).astype(BF16)
    cs = _dot01_left(tril_b, lf_ref[0]) + carry_scr[...]
    carry_scr[...] = cs[L - 1:L, :]
    o_ref[0] = cs.T[0:rows, :] * LOG2E


def cumsum_t(lf, rows):
    b, t, _ = lf.shape
    L = min(512, t)
    return pl.pallas_call(
        functools.partial(_cumsum_t_kernel, rows=rows),
        grid=(b, t // L),
        in_specs=[pl.BlockSpec((1, L, LANES), lambda i, j: (i, j, 0))],
        out_specs=pl.BlockSpec((1, rows, L), lambda i, j: (i, 0, j)),
        out_shape=jax.ShapeDtypeStruct((b, rows, t), F32),
        scratch_shapes=[pltpu.VMEM((1, LANES), F32)],
        compiler_params=_cparams(("parallel", "arbitrary")),
        name="cumsum_t",
    )(lf)


def _fox_prompt_kernel(qi_ref, ki_ref, q_ref, k_ref, v_ref, c_ref, o_ref, qm_scr, m_scr, acc_scr, *, tq, tk):
    step = pl.program_id(2)
    qi = qi_ref[step]
    ki = ki_ref[step]
    last_k = (qi * tq + tq - 1) // tk
    heads = range(HEADS_PER_STEP)

    @pl.when(ki == 0)
    def _():
        _masked_q(q_ref, qm_scr)
        m_scr[...] = jnp.full(m_scr.shape, NEG_INF, F32)
        acc_scr[...] = jnp.zeros(acc_scr.shape, F32)

    def body(r0, r1, nk, masked):
        rows = slice(r0, r1)
        kb = k_ref[0, 0:nk, :]
        vb = v_ref[0, 0:nk, :]
        vb_ext = jnp.concatenate([vb, jnp.ones_like(vb)], axis=1)
        if masked:
            row = qi * tq + r0 + _iota((r1 - r0, nk), 0)
            col = ki * tk + _iota((r1 - r0, nk), 1)
            keep = col <= row
        scores = []
        for h in heads:
            s = lax.dot_general(qm_scr[h, rows, :], kb, NT_DIMS, preferred_element_type=F32) - c_ref[0, h:h + 1, 0:nk]
            scores.append(jnp.where(keep, s, NEG_INF) if masked else s)
        probs, alphas = [], []
        for h in heads:
            m_prev = m_scr[h, rows, :]
            m_new = jnp.maximum(m_prev, jnp.max(scores[h], axis=-1, keepdims=True))
            alphas.append(jnp.exp2(m_prev - m_new))
            probs.append(jnp.exp2(scores[h] - jnp.tile(m_new, (1, nk // LANES))).astype(BF16))
            m_scr[h, rows, :] = m_new
        for h in heads:
            acc_scr[h, rows, :] = (jnp.tile(alphas[h], (1, 2)) * acc_scr[h, rows, :]
                                   + jnp.dot(probs[h], vb_ext, preferred_element_type=F32))

    on_diagonal = (ki + 1) * tk - 1 > qi * tq

    @pl.when(on_diagonal)
    def _():
        if tq == tk and tq % (2 * LANES) == 0:
            body(0, tq // 2, tk // 2, True)
            body(tq // 2, tq, tk, True)
        else:
            body(0, tq, tk, True)

    @pl.when(jnp.logical_not(on_diagonal))
    def _():
        body(0, tq, tk, False)

    @pl.when(ki == last_k)
    def _():
        _write_heads(o_ref, [acc_scr[h, :, 0:LANES] for h in heads],
                     [1.0 / acc_scr[h, :, LANES:2 * LANES] for h in heads])


def fox_prompt(q, kb, vb, cum_t, tq=512, tk=512):
    b, t, _ = q.shape
    tq, tk = min(tq, t), min(tk, t)
    npair = N_KV // 2
    qi_l, ki_l = [], []
    for qi in range(t // tq):
        for ki in range((qi * tq + tq - 1) // tk + 1):
            qi_l.append(qi)
            ki_l.append(ki)
    qi_arr = jnp.asarray(qi_l, jnp.int32)
    ki_arr = jnp.asarray(ki_l, jnp.int32)
    qspec = pl.BlockSpec((1, tq, SLABS_PER_STEP * LANES), lambda bi, j, s, qi, ki: (bi, qi[s], j))
    kspec = pl.BlockSpec((1, tk, LANES), lambda bi, j, s, qi, ki: (bi, ki[s], j))
    cspec = pl.BlockSpec((1, HEADS_PER_STEP, tk), lambda bi, j, s, qi, ki: (bi, j, ki[s]))
    grid_spec = pltpu.PrefetchScalarGridSpec(
        num_scalar_prefetch=2,
        grid=(b, npair, len(qi_l)),
        in_specs=[qspec, kspec, kspec, cspec],
        out_specs=qspec,
        scratch_shapes=[pltpu.VMEM((HEADS_PER_STEP, tq, LANES), BF16),
                        pltpu.VMEM((HEADS_PER_STEP, tq, LANES), F32),
                        pltpu.VMEM((HEADS_PER_STEP, tq, 2 * LANES), F32)],
    )
    return pl.pallas_call(
        functools.partial(_fox_prompt_kernel, tq=tq, tk=tk),
        grid_spec=grid_spec,
        out_shape=jax.ShapeDtypeStruct(q.shape, BF16),
        compiler_params=_cparams(("parallel", "parallel", "arbitrary")),
        name="fox_prompt",
    )(qi_arr, ki_arr, q, kb, vb, cum_t)


def _xattn_kernel(q_ref, k_ref, v_ref, o_ref, *, nb, mem, interleaved):
    scale = LOG2E * XA_DIM ** -0.5
    pairs = [(bi, h) for bi in range(nb) for h in range(XA_HEADS)]
    lanes = lambda h: slice(h * XA_DIM, (h + 1) * XA_DIM)

    def kv_head(ref, bi, h):
        if interleaved:
            return ref[bi, pl.ds(h, mem, stride=XA_HEADS), :].astype(BF16)
        return ref[bi, :, lanes(h)].astype(BF16)

    scores = [lax.dot_general(q_ref[bi, :, lanes(h)], kv_head(k_ref, bi, h), NT_DIMS,
                              preferred_element_type=F32) * scale for bi, h in pairs]
    probs, inv_l = [], []
    for s in scores:
        p = jnp.exp2(s - jnp.max(s, axis=-1, keepdims=True))
        inv_l.append(1.0 / jnp.sum(p, axis=-1, keepdims=True))
        probs.append(p.astype(BF16))
    for (bi, h), p, il in zip(pairs, probs, inv_l):
        o = jnp.dot(p, kv_head(v_ref, bi, h), preferred_element_type=F32) * il
        o_ref[bi, :, lanes(h)] = o.astype(o_ref.dtype)


def xattn(q, mk, mv, tq, nb, kv_first=0):
    b, t, _ = q.shape
    interleaved = mk.shape[2] == XA_DIM
    mem = mk.shape[1] // XA_HEADS if interleaved else mk.shape[1]
    assert kv_first % nb == 0
    off = kv_first // nb
    qspec = pl.BlockSpec((nb, tq, XA_WIDTH), lambda i, j: (i, j, 0))
    mspec = pl.BlockSpec((nb,) + mk.shape[1:], lambda i, j: (i + off, 0, 0))
    return pl.pallas_call(
        functools.partial(_xattn_kernel, nb=nb, mem=mem, interleaved=interleaved),
        grid=(b // nb, t // tq),
        in_specs=[qspec, mspec, mspec],
        out_specs=qspec,
        out_shape=jax.ShapeDtypeStruct(q.shape, BF16),
        compiler_params=_cparams(("parallel", "arbitrary")),
        name="xattn",
    )(q, mk, mv)


def _block_diag_q(q, nb, t):
    q4 = q.reshape(nb, t * N_HEADS, HEAD_DIM)
    qt = jnp.tile(q4, (1, 1, N_KV))
    row_kv = (np.arange(t * N_HEADS) % N_HEADS) // GROUP
    keep = jnp.asarray(row_kv[:, None] == (np.arange(N_KV * HEAD_DIM) // HEAD_DIM)[None, :])
    return jnp.where(keep[None], qt * jnp.asarray(HEAD_DIM ** -0.5, BF16), jnp.zeros((), BF16))


def _diag_extract(acc, rows):
    row_kv = (_iota((rows, HEAD_DIM), 0) % N_HEADS) // GROUP
    out = jnp.zeros((rows, HEAD_DIM), F32)
    for kv in range(N_KV):
        out = out + jnp.where(row_kv == kv, acc[:, kv * HEAD_DIM:(kv + 1) * HEAD_DIM], 0.0)
    return out


NEW_PAD = 16


def _swa_sample_kernel(q_ref, ck_ref, cv_ref, kn_ref, vn_ref, knt_ref, vnt_ref, sink_ref, o_ref, ok_ref, ov_ref,
                       *, nb, t):
    w = WINDOW
    rows = t * N_HEADS
    tok_c = _iota((rows, w), 0) // N_HEADS
    keep_c = _iota((rows, w), 1) > tok_c
    keep_n = _iota((rows, NEW_PAD), 1) <= _iota((rows, NEW_PAD), 0) // N_HEADS
    sink = sink_ref[:, 0:1]
    place = (_iota((NEW_PAD, w), 1) == _iota((NEW_PAD, w), 0) + (w - t)).astype(BF16)
    old_slots = _iota((N_KV * HEAD_DIM, w), 1) < w - t
    for bi in range(nb):
        ck = ck_ref[bi]
        cv = cv_ref[bi]
        q = q_ref[bi]
        s_c = jnp.where(keep_c, jnp.dot(q, ck.astype(BF16), preferred_element_type=F32), NEG_INF)
        s_n = jnp.where(keep_n, lax.dot_general(q, kn_ref[bi].astype(BF16), NT_DIMS, preferred_element_type=F32),
                        NEG_INF)
        m = jnp.maximum(jnp.maximum(jnp.max(s_c, axis=-1, keepdims=True), jnp.max(s_n, axis=-1, keepdims=True)), sink)
        p_c = jnp.exp(s_c - m)
        p_n = jnp.exp(s_n - m)
        l = jnp.sum(p_c, axis=-1, keepdims=True) + jnp.sum(p_n, axis=-1, keepdims=True) + jnp.exp(sink - m)
        acc = (lax.dot_general(p_c.astype(BF16), cv.astype(BF16), NT_DIMS, preferred_element_type=F32)
               + jnp.dot(p_n.astype(BF16), vn_ref[bi].astype(BF16), preferred_element_type=F32))
        o_ref[bi] = (_diag_extract(acc, rows) * (1.0 / l)).astype(o_ref.dtype)
        ok_ref[bi] = jnp.where(old_slots, pltpu.roll(ck, w - t, 1), _dot01_right(knt_ref[bi], place))
        ov_ref[bi] = jnp.where(old_slots, pltpu.roll(cv, w - t, 1), _dot01_right(vnt_ref[bi], place))


def swa_sample(qbd, ck_t, cv_t, kn, vn, sink, t, nb=8):
    b = qbd.shape[0]
    w = WINDOW
    rows = t * N_HEADS
    sink_rows = jnp.broadcast_to(jnp.tile(sink.astype(F32), t).reshape(rows, 1), (rows, LANES))
    bspec = lambda s1, s2: pl.BlockSpec((nb, s1, s2), lambda i: (i, 0, 0))
    kvw = N_KV * HEAD_DIM
    knt, vnt = jnp.swapaxes(kn, 1, 2), jnp.swapaxes(vn, 1, 2)
    return pl.pallas_call(
        functools.partial(_swa_sample_kernel, nb=nb, t=t),
        grid=(b // nb,),
        in_specs=[bspec(rows, kvw), bspec(kvw, w), bspec(kvw, w), bspec(NEW_PAD, kvw), bspec(NEW_PAD, kvw),
                  bspec(kvw, NEW_PAD), bspec(kvw, NEW_PAD), _const_spec((rows, LANES))],
        out_specs=[bspec(rows, HEAD_DIM), bspec(kvw, w), bspec(kvw, w)],
        out_shape=[jax.ShapeDtypeStruct((b, rows, HEAD_DIM), BF16),
                   jax.ShapeDtypeStruct((b, kvw, w), F32), jax.ShapeDtypeStruct((b, kvw, w), F32)],
        compiler_params=_cparams(("parallel",)),
        name="swa_sample",
    )(qbd, ck_t, cv_t, kn, vn, knt, vnt, sink_rows)


FOX_PAGES_PER_STEP = 64


def _fox_sample_kernel(pt_ref, q_ref, kn_ref, vn_ref, lfn_ref, kc_hbm, vc_hbm, lf_hbm, o_ref,
                       kbuf, vbuf, lbuf, sem, carry_scr, m_scr, l_scr, acc_scr,
                       *, t, n_pages, pg):
    b = pl.program_id(0)
    c = pl.program_id(1)
    nch = pl.num_programs(1)
    step = b * nch + c
    total = pl.num_programs(0) * nch
    rows = t * N_HEADS
    nh = N_HEADS

    def copies(stp, slot):
        base = (stp // nch) * n_pages + (stp % nch) * pg
        out = []
        for p in range(pg):
            page = pt_ref[base + p]
            out.append(pltpu.make_async_copy(kc_hbm.at[page], kbuf.at[slot, :, pl.ds(p * PAGE, PAGE)], sem.at[slot, 0]))
            out.append(pltpu.make_async_copy(vc_hbm.at[page], vbuf.at[slot, :, pl.ds(p * PAGE, PAGE)], sem.at[slot, 1]))
            out.append(pltpu.make_async_copy(lf_hbm.at[page], lbuf.at[slot, pl.ds(p * nh, nh)], sem.at[slot, 2]))
        return out

    slot = step % 2

    @pl.when(step == 0)
    def _():
        for cp in copies(step, slot):
            cp.start()

    @pl.when(step + 1 < total)
    def _():
        for cp in copies(step + 1, 1 - slot):
            cp.start()

    @pl.when(c == 0)
    def _():
        carry_scr[...] = jnp.zeros(carry_scr.shape, F32)
        m_scr[...] = jnp.full(m_scr.shape, NEG_INF, F32)
        l_scr[...] = jnp.zeros(l_scr.shape, F32)
        acc_scr[...] = jnp.zeros(acc_scr.shape, F32)

    for cp in copies(step, slot):
        cp.wait()

    q = q_ref[0]
    upper_b = (_iota((PAGE, PAGE), 0) <= _iota((PAGE, PAGE), 1)).astype(BF16)

    def online(s, pv):
        m_prev = m_scr[...]
        m_new = jnp.maximum(m_prev, jnp.max(s, axis=-1, keepdims=True))
        alpha = jnp.exp(m_prev - m_new)
        p = jnp.exp(s - m_new)
        l_scr[...] = alpha * l_scr[...] + jnp.sum(p, axis=-1, keepdims=True)
        acc_scr[...] = alpha * acc_scr[...] + pv(p.astype(BF16))
        m_scr[...] = m_new

    lf = lbuf[slot]
    cl = _dot01_right(lf, upper_b)
    tot = jnp.broadcast_to(cl[:, PAGE - 1:PAGE], cl.shape)
    r = _iota((pg * nh, pg * nh), 0)
    cc = _iota((pg * nh, pg * nh), 1)
    earlier_b = ((r % nh == cc % nh) & (cc // nh < r // nh)).astype(BF16)
    cum = cl + _dot01_left(earlier_b, tot) + jnp.concatenate([carry_scr[...]] * pg, axis=0)
    carry_new = jnp.broadcast_to(cum[(pg - 1) * nh:pg * nh, PAGE - 1:PAGE], (nh, LANES))
    carry_scr[...] = carry_new
    bias16 = jnp.concatenate([cum[p * nh:(p + 1) * nh, :] for p in range(pg)], axis=1)
    bias = jnp.concatenate([bias16] * t, axis=0)

    kc = kbuf[slot].astype(BF16)
    vc = vbuf[slot].astype(BF16)
    s = jnp.dot(q, kc, preferred_element_type=F32) - bias
    online(s, lambda p: lax.dot_general(p, vc, NT_DIMS, preferred_element_type=F32))

    @pl.when(c == nch - 1)
    def _():
        zpad = jnp.zeros((PAGE - NEW_PAD, N_KV * HEAD_DIM), BF16)
        knew = jnp.concatenate([kn_ref[0].astype(BF16), zpad], axis=0)
        vnew = jnp.concatenate([vn_ref[0].astype(BF16), zpad], axis=0)
        cum_n = _dot01_right(lfn_ref[0], upper_b) + carry_new
        sn = lax.dot_general(q, knew, NT_DIMS, preferred_element_type=F32)
        sn = sn - jnp.concatenate([cum_n] * t, axis=0)
        tok = _iota((rows, PAGE), 0) // nh
        sn = jnp.where(_iota((rows, PAGE), 1) <= tok, sn, NEG_INF)
        online(sn, lambda p: jnp.dot(p, vnew, preferred_element_type=F32))
        o_ref[0] = (_diag_extract(acc_scr[...], rows) * (1.0 / l_scr[...])).astype(o_ref.dtype)


def fox_sample(page_table, qbd, kn, vn, lfn_t, kcache, vcache, lf_t_cache, t, pg=FOX_PAGES_PER_STEP):
    b, n_pages = page_table.shape
    pg = min(pg, n_pages)
    assert n_pages % pg == 0
    rows = t * N_HEADS
    kvw = N_KV * HEAD_DIM
    bspec = lambda s1, s2: pl.BlockSpec((1, s1, s2), lambda i, j, pt: (i, 0, 0))
    any_spec = pl.BlockSpec(memory_space=pl.ANY)
    grid_spec = pltpu.PrefetchScalarGridSpec(
        num_scalar_prefetch=1,
        grid=(b, n_pages // pg),
        in_specs=[bspec(rows, kvw), bspec(NEW_PAD, kvw), bspec(NEW_PAD, kvw), bspec(N_HEADS, LANES),
                  any_spec, any_spec, any_spec],
        out_specs=bspec(rows, HEAD_DIM),
        scratch_shapes=[pltpu.VMEM((2, kvw, pg * PAGE), F32), pltpu.VMEM((2, kvw, pg * PAGE), F32),
                        pltpu.VMEM((2, pg * N_HEADS, LANES), F32), pltpu.SemaphoreType.DMA((2, 3)),
                        pltpu.VMEM((N_HEADS, LANES), F32),
                        pltpu.VMEM((rows, 1), F32), pltpu.VMEM((rows, 1), F32), pltpu.VMEM((rows, kvw), F32)],
    )
    return pl.pallas_call(
        functools.partial(_fox_sample_kernel, t=t, n_pages=n_pages, pg=pg),
        grid_spec=grid_spec,
        out_shape=jax.ShapeDtypeStruct((b, rows, HEAD_DIM), BF16),
        compiler_params=_cparams(("arbitrary", "arbitrary")),
        name="fox_sample",
    )(page_table.reshape(-1), qbd, kn, vn, lfn_t, kcache, vcache, lf_t_cache)


def _pad_cols(w, width):
    return jnp.pad(w, ((0, 0), (0, width - w.shape[1])))


def kernel(x_prompt, x_sample, state_ssm, state_conv, cache_swa_k, cache_swa_v, cache_fox_k, cache_fox_v,
           cache_fox_logf, cache_mem_k, cache_mem_v, page_table, mem_prompt, norm_mix, norm_xa, norm_mem,
           norm_ffn, w_in_even, conv_w, conv_b, dt_bias, a_log, d_skip, ssm_norm, swa_sink, w_out_even,
           w_in_odd, fox_fb, w_out_odd, w_xq, w_xk, w_xv, w_xo, w_ffn_in, w_ffn_out, norm_final):
    bp, seq, d = x_prompt.shape
    bs, ts, _ = x_sample.shape
    depth = norm_mix.shape[0]
    mem_len = mem_prompt.shape[1]
    kvw = N_KV * HEAD_DIM
    qw = N_HEADS * HEAD_DIM
    xp = x_prompt.reshape(bp * seq, d)
    xs = x_sample.reshape(bs * ts, d)
    mem2 = mem_prompt.reshape(bp * mem_len, d)
    w_ffn_in_b = w_ffn_in.astype(BF16)
    w_ffn_out_b = w_ffn_out.astype(BF16)
    outs = {k: [] for k in ("p_ssm", "p_conv", "p_swk", "p_swv", "p_fk", "p_fv", "p_fl", "p_mk", "p_mv",
                            "s_ssm", "s_conv", "s_swk", "s_swv", "s_fk", "s_fv", "s_fl")}

    for l in range(depth):
        li = l // 2
        if l % 2 == 0:
            w = w_in_even[li]
            o_z, o_xbc, o_dt = 0, SSM_WIDTH, SSM_WIDTH + CONV_DIM
            o_q = o_dt + SSM_HEADS
            o_k, o_v = o_q + qw, o_q + qw + kvw
            wz, wxbc, wdt = w[:, o_z:o_xbc], w[:, o_xbc:o_dt], _pad_cols(w[:, o_dt:o_q], LANES)
            wq, wk, wv = w[:, o_q:o_k], w[:, o_k:o_v], w[:, o_v:o_v + kvw]
            c_xbc, c_q, c_k, c_v, c_dt = SSM_WIDTH, SSM_WIDTH + CONV_DIM, SSM_WIDTH + CONV_DIM + qw, \
                SSM_WIDTH + CONV_DIM + qw + kvw, SSM_WIDTH + CONV_DIM + qw + 2 * kvw
            n_tot = c_dt + LANES
            wo = w_out_even[li]
            wo_ssm = wo[:SSM_WIDTH].astype(BF16)
            wo_att = wo[SSM_WIDTH:]
            ssm_params = (conv_w[li], conv_b[li], dt_bias[li], a_log[li], d_skip[li], ssm_norm[li])

            w_p = jnp.concatenate([wz, wxbc, wq[:, COL_PERM], wk, wv, wdt], axis=1).astype(BF16)
            segs_p = [(0, c_xbc, (F32,), None), (c_xbc, c_q, (F32,), None), (c_q, c_k, (BF16,), Q_SCALE),
                      (c_k, c_v, (F32, BF16), None), (c_v, c_dt, (F32, BF16), None), (c_dt, n_tot, (F32,), None)]
            z, xbc, q, k, kb, v, vb, dtr = norm_proj(xp, norm_mix[l], w_p, segs_p)
            xbc3 = xbc.reshape(bp, seq, CONV_DIM)
            y_ssm, h = ssd(xbc3, z.reshape(bp, seq, SSM_WIDTH), dtr.reshape(bp, seq, LANES),
                           jnp.zeros((bp, 8, CONV_DIM), F32), None, *ssm_params, L=SSM_CHUNK, valid=SSM_CHUNK,
                           nb=math.gcd(bp, 4))
            y_att = swa_prompt(q.reshape(bp, seq, qw), kb.reshape(bp, seq, kvw), vb.reshape(bp, seq, kvw),
                               swa_sink[li][HEAD_PERM])
            mix_p = ([y_ssm.reshape(bp * seq, SSM_WIDTH), y_att.reshape(bp * seq, qw)],
                     [wo_ssm, wo_att[COL_PERM].astype(BF16)])
            last_w = lambda a: a.reshape(bp, seq, kvw)[:, seq - WINDOW:].reshape(bp, WINDOW, N_KV, HEAD_DIM)
            outs["p_ssm"].append(h.reshape(bp, SSM_HEADS, SSM_P, SSM_N))
            outs["p_conv"].append(xbc3[:, seq - (CONV_K - 1):])
            outs["p_swk"].append(last_w(k))
            outs["p_swv"].append(last_w(v))

            w_s = jnp.concatenate([wz, wxbc, wq, wk, wv, wdt], axis=1).astype(BF16)
            segs_s = [(0, c_xbc, (F32,), None), (c_xbc, c_q, (F32,), None), (c_q, c_k, (BF16,), None),
                      (c_k, c_v, (F32,), None), (c_v, c_dt, (F32,), None), (c_dt, n_tot, (F32,), None)]
            z, xbc, q, k, v, dtr = norm_proj(xs, norm_mix[l], w_s, segs_s)
            ls = NEW_PAD
            pad_t = lambda a: jnp.pad(a.reshape(bs, ts, a.shape[-1]), ((0, 0), (0, ls - ts), (0, 0)))
            xbc3 = xbc.reshape(bs, ts, CONV_DIM)
            tail = jnp.pad(state_conv[li].astype(F32), ((0, 0), (8 - (CONV_K - 1), 0), (0, 0)))
            y_ssm, h = ssd(pad_t(xbc), pad_t(z), pad_t(dtr), tail, state_ssm[li].reshape(bs, SSM_WIDTH, SSM_N),
                           *ssm_params, L=ls, valid=ts, nb=math.gcd(bs, 8))
            y_ssm = y_ssm[:, :ts].reshape(bs * ts, SSM_WIDTH)
            kn, vn = pad_t(k), pad_t(v)
            win_t = lambda c: jnp.transpose(c[li].astype(F32), (0, 2, 3, 1)).reshape(bs, kvw, WINDOW)
            o_att, new_k, new_v = swa_sample(_block_diag_q(q, bs, ts), win_t(cache_swa_k), win_t(cache_swa_v),
                                             kn, vn, swa_sink[li], ts)
            win_back = lambda c: jnp.transpose(c.reshape(bs, N_KV, HEAD_DIM, WINDOW), (0, 3, 1, 2))
            mix_s = ([y_ssm, o_att.reshape(bs * ts, qw)], [wo_ssm, wo_att.astype(BF16)])
            outs["s_ssm"].append(h.reshape(bs, SSM_HEADS, SSM_P, SSM_N))
            outs["s_conv"].append(jnp.concatenate([state_conv[li].astype(F32), xbc3], axis=1)[:, -(CONV_K - 1):])
            outs["s_swk"].append(win_back(new_k))
            outs["s_swv"].append(win_back(new_v))
        else:
            w = w_in_odd[li]
            wq, wk, wv, wf = w[:, :qw], w[:, qw:qw + kvw], w[:, qw + kvw:qw + 2 * kvw], w[:, qw + 2 * kvw:]
            c_k, c_v, c_f = qw, qw + kvw, qw + 2 * kvw
            n_tot = c_f + LANES
            fb = fox_fb[li].astype(F32)

            wf_p = _pad_cols(jnp.concatenate([wf[:, HEAD_PERM], wf], axis=1), LANES)
            fb_p = jnp.pad(jnp.concatenate([fb[HEAD_PERM], fb]), (0, LANES - 2 * N_HEADS)).reshape(1, LANES)
            w_p = jnp.concatenate([wq[:, COL_PERM], wk, wv, wf_p], axis=1).astype(BF16)
            segs_p = [(0, c_k, (BF16,), Q_SCALE), (c_k, c_v, ("T", BF16), None), (c_v, c_f, ("T", BF16), None),
                      (c_f, n_tot, (F32,), "logsig")]
            q, k_t, kb, v_t, vb, lf = norm_proj(xp, norm_mix[l], w_p, segs_p, aux=fb_p, rows_per_batch=seq)
            lf3 = lf.reshape(bp, seq, LANES)
            cum_t = cumsum_t(lf3, N_HEADS)
            y = fox_prompt(q.reshape(bp, seq, qw), kb.reshape(bp, seq, kvw), vb.reshape(bp, seq, kvw), cum_t)
            mix_p = ([y.reshape(bp * seq, qw)], [w_out_odd[li][COL_PERM].astype(BF16)])
            untr = lambda a: jnp.transpose(a.reshape(bp, N_KV, HEAD_DIM, seq), (0, 3, 1, 2))
            outs["p_fk"].append(untr(k_t))
            outs["p_fv"].append(untr(v_t))
            outs["p_fl"].append(lf3[:, :, N_HEADS:2 * N_HEADS])

            w_s = jnp.concatenate([wq, wk, wv, _pad_cols(wf, LANES)], axis=1).astype(BF16)
            fb_s = jnp.pad(fb, (0, LANES - N_HEADS)).reshape(1, LANES)
            segs_s = [(0, c_k, (BF16,), None), (c_k, c_v, (F32,), None), (c_v, c_f, (F32,), None),
                      (c_f, n_tot, (F32,), "logsig")]
            q, k, v, lf = norm_proj(xs, norm_mix[l], w_s, segs_s, aux=fb_s)
            lf_new = lf[:, :N_HEADS].reshape(bs, ts, N_HEADS)
            pad_t = lambda a: jnp.pad(a.reshape(bs, ts, a.shape[-1]), ((0, 0), (0, NEW_PAD - ts), (0, 0)))
            lfn_t = jnp.pad(jnp.swapaxes(lf_new, 1, 2), ((0, 0), (0, 0), (0, LANES - ts)))
            pool = cache_fox_k.shape[1]
            page_t = lambda c: jnp.transpose(c[li].astype(F32), (0, 2, 3, 1)).reshape(pool, kvw, PAGE)
            y = fox_sample(page_table, _block_diag_q(q, bs, ts), pad_t(k), pad_t(v), lfn_t,
                           page_t(cache_fox_k), page_t(cache_fox_v),
                           jnp.swapaxes(cache_fox_logf[li].astype(F32), 1, 2), ts)
            mix_s = ([y.reshape(bs * ts, qw)], [w_out_odd[li].astype(BF16)])
            outs["s_fk"].append(k.reshape(bs, ts, N_KV, HEAD_DIM))
            outs["s_fv"].append(v.reshape(bs, ts, N_KV, HEAD_DIM))
            outs["s_fl"].append(lf_new)

        wkv = jnp.concatenate([w_xk[l], w_xv[l]], axis=1).astype(BF16)
        mk, mv = norm_proj(mem2, norm_mem[l], wkv, [(0, XA_WIDTH, (F32,), None), (XA_WIDTH, 2 * XA_WIDTH, (F32,), None)])
        wq_b = w_xq[l].astype(BF16)
        wo_b = w_xo[l].astype(BF16)
        xp, qx = matmul_residual(*mix_p, xp, proj=(norm_xa[l], wq_b))
        ox_p = xattn(qx.reshape(bp, seq, XA_WIDTH), mk.reshape(bp, mem_len, XA_WIDTH),
                     mv.reshape(bp, mem_len, XA_WIDTH), tq=512, nb=1).reshape(bp * seq, XA_WIDTH)
        xs, qx = matmul_residual(*mix_s, xs, proj=(norm_xa[l], wq_b))
        qx = jnp.pad(qx.reshape(bs, ts, XA_WIDTH), ((0, 0), (0, NEW_PAD - ts), (0, 0)))
        all_layers = lambda c: c.astype(F32).reshape(depth * bs, mem_len * XA_HEADS, XA_DIM)
        ox_s = xattn(qx, all_layers(cache_mem_k), all_layers(cache_mem_v), tq=NEW_PAD, nb=4, kv_first=l * bs)
        ox_s = ox_s[:, :ts].reshape(bs * ts, XA_WIDTH)
        outs["p_mk"].append(mk.reshape(bp, mem_len, XA_HEADS, XA_DIM))
        outs["p_mv"].append(mv.reshape(bp, mem_len, XA_HEADS, XA_DIM))

        gf = norm_final if l == depth - 1 else None
        xp = ffn(xp, norm_ffn[l], w_ffn_in_b, w_ffn_out_b, l, g_final=gf, pre=(ox_p, wo_b))
        xs = ffn(xs, norm_ffn[l], w_ffn_in_b, w_ffn_out_b, l, g_final=gf, pre=(ox_s, wo_b))

    st = lambda key: jnp.stack(outs[key])
    return (xp.reshape(bp, seq, d), xs.reshape(bs, ts, d),
            st("p_ssm"), st("p_conv"), st("p_swk"), st("p_swv"), st("p_fk"), st("p_fv"), st("p_fl"),
            st("p_mk"), st("p_mv"), st("s_ssm"), st("s_conv"), st("s_swk"), st("s_swv"),
            st("s_fk"), st("s_fv"), st("s_fl"))
```

```python
import functools
import math

import numpy as np
import jax
import jax.numpy as jnp
from jax import lax
from jax.experimental import pallas as pl
from jax.experimental.pallas import tpu as pltpu

F32 = jnp.float32
BF16 = jnp.bfloat16

D_MODEL = 1024
HEAD_DIM = 64
N_HEADS = 16
N_KV = 4
SSM_HEADS = 16
SSM_P = 64
SSM_WIDTH = 1024
SSM_N = 128
SSM_GROUPS = 2
SSM_CHUNK = 128
CONV_K = 4
CONV_DIM = 1536
WINDOW = 128
PAGE = 128
XA_HEADS = 4
XA_DIM = 128
XA_WIDTH = 512
FFN_H = 2816
RMS_EPS = 1e-6
NEG_INF = -1e30

V7X_VMEM_BYTES = 64 * 1024 * 1024
VMEM_LIMIT = V7X_VMEM_BYTES - 8 * 1024 * 1024
LANES = 128

NT_DIMS = (((1,), (1,)), ((), ()))


def _cparams(sem):
    return pltpu.CompilerParams(dimension_semantics=sem, vmem_limit_bytes=VMEM_LIMIT)


def _const_spec(shape):
    nd = len(shape)
    return pl.BlockSpec(shape, lambda *_: (0,) * nd, pipeline_mode=pl.Buffered(1))


def _softplus(x):
    return jnp.maximum(x, 0.0) + jnp.log1p(jnp.exp(-jnp.abs(x)))


def _silu(x):
    return x * (1.0 / (1.0 + jnp.exp(-x)))


def _split3(x):
    hi = x.astype(BF16)
    r1 = x - hi.astype(F32)
    mid = r1.astype(BF16)
    lo = (r1 - mid.astype(F32)).astype(BF16)
    return hi, mid, lo


def _dot01_left(m01, x):
    hi, mid, lo = _split3(x)
    d = lambda b: jnp.dot(m01, b, preferred_element_type=F32)
    return (d(lo) + d(mid)) + d(hi)


def _dot01_right(x, m01):
    hi, mid, lo = _split3(x)
    d = lambda a: jnp.dot(a, m01, preferred_element_type=F32)
    return (d(lo) + d(mid)) + d(hi)


def _iota(shape, dim):
    return lax.broadcasted_iota(jnp.int32, shape, dim)


def _norm_proj_kernel(*refs, segs, has_aux):
    x_ref, g_ref, w_ref = refs[:3]
    pos = 3
    aux_ref = None
    if has_aux:
        aux_ref = refs[3]
        pos = 4
    out_refs = refs[pos:]
    x = x_ref[...]
    ms = jnp.mean(x * x, axis=-1, keepdims=True)
    xn = (x * lax.rsqrt(ms + RMS_EPS)) * g_ref[...]
    xb = xn.astype(BF16)
    oi = 0
    for (c0, c1, dtypes, act) in segs:
        y = jnp.dot(xb, w_ref[:, c0:c1], preferred_element_type=F32)
        if act == "logsig":
            y = -_softplus(-(y + aux_ref[...]))
        elif act is not None:
            y = y * act
        for dt in dtypes:
            if dt == "T":
                out_refs[oi][0] = y.T
            else:
                out_refs[oi][...] = y.astype(dt)
            oi += 1


def norm_proj(x, g, w, segs, aux=None, tm=512, rows_per_batch=None):
    m, d = x.shape
    tm = min(tm, m)
    assert m % tm == 0
    n = w.shape[1]
    tiles_per_b = (rows_per_batch or m) // tm
    in_specs = [pl.BlockSpec((tm, d), lambda i: (i, 0)), _const_spec((1, d)), _const_spec((d, n))]
    args = [x, g.reshape(1, d).astype(F32), w]
    if aux is not None:
        in_specs.append(_const_spec(aux.shape))
        args.append(aux)
    out_shape, out_specs = [], []
    for (c0, c1, dtypes, _) in segs:
        for dt in dtypes:
            if dt == "T":
                out_shape.append(jax.ShapeDtypeStruct((m // rows_per_batch, c1 - c0, rows_per_batch), F32))
                out_specs.append(pl.BlockSpec((1, c1 - c0, tm), lambda i: (i // tiles_per_b, 0, i % tiles_per_b)))
            else:
                out_shape.append(jax.ShapeDtypeStruct((m, c1 - c0), dt))
                out_specs.append(pl.BlockSpec((tm, c1 - c0), lambda i: (i, 0)))
    return pl.pallas_call(
        functools.partial(_norm_proj_kernel, segs=tuple(segs), has_aux=aux is not None),
        grid=(m // tm,),
        in_specs=in_specs,
        out_specs=out_specs,
        out_shape=out_shape,
        compiler_params=_cparams(("parallel",)),
        name="norm_proj",
    )(*args)


def _mm_res_kernel(*refs, n, proj):
    a_refs, w_refs = refs[:n], refs[n:2 * n]
    res_ref = refs[2 * n]
    acc = res_ref[...]
    for a_ref, w_ref in zip(a_refs, w_refs):
        acc = acc + jnp.dot(a_ref[...].astype(BF16), w_ref[...], preferred_element_type=F32)
    if proj:
        g_ref, wp_ref, o_ref, p_ref = refs[2 * n + 1:]
        ms = jnp.mean(acc * acc, axis=-1, keepdims=True)
        xb = ((acc * lax.rsqrt(ms + RMS_EPS)) * g_ref[...]).astype(BF16)
        p_ref[...] = jnp.dot(xb, wp_ref[...], preferred_element_type=F32).astype(p_ref.dtype)
    else:
        o_ref = refs[2 * n + 1]
    o_ref[...] = acc


def matmul_residual(a_list, w_list, res, proj=None, tm=512):
    m, d = res.shape
    tm = min(tm, m)
    assert m % tm == 0
    n = len(a_list)
    in_specs = [pl.BlockSpec((tm, a.shape[1]), lambda i: (i, 0)) for a in a_list]
    in_specs += [_const_spec(w.shape) for w in w_list]
    in_specs.append(pl.BlockSpec((tm, d), lambda i: (i, 0)))
    args = [*a_list, *w_list, res]
    out_specs = pl.BlockSpec((tm, d), lambda i: (i, 0))
    out_shape = jax.ShapeDtypeStruct((m, d), F32)
    if proj is not None:
        g, wp = proj
        in_specs += [_const_spec((1, d)), _const_spec(wp.shape)]
        args += [g.reshape(1, d).astype(F32), wp]
        out_specs = [out_specs, pl.BlockSpec((tm, wp.shape[1]), lambda i: (i, 0))]
        out_shape = [out_shape, jax.ShapeDtypeStruct((m, wp.shape[1]), BF16)]
    return pl.pallas_call(
        functools.partial(_mm_res_kernel, n=n, proj=proj is not None),
        grid=(m // tm,),
        in_specs=in_specs,
        out_specs=out_specs,
        out_shape=out_shape,
        compiler_params=_cparams(("parallel",)),
        name="matmul_residual",
    )(*args)


FFN_CHUNK = 512


def _ffn_kernel(*refs, final_norm, pre):
    refs = list(refs)
    x_ref = refs.pop(0)
    x = x_ref[...]
    if pre:
        a_ref, wa_ref = refs.pop(0), refs.pop(0)
        x = x + jnp.dot(a_ref[...], wa_ref[...], preferred_element_type=F32)
    if final_norm:
        g_ref, wi_ref, wo_ref, gf_ref, o_ref, acc_ref = refs
    else:
        g_ref, wi_ref, wo_ref, o_ref, acc_ref = refs
    ms = jnp.mean(x * x, axis=-1, keepdims=True)
    xb = ((x * lax.rsqrt(ms + RMS_EPS)) * g_ref[...]).astype(BF16)
    hdim = wo_ref.shape[0]
    acc_ref[...] = x
    for c0 in range(0, hdim, FFN_CHUNK):
        c1 = min(c0 + FFN_CHUNK, hdim)
        gt = jnp.dot(xb, wi_ref[:, c0:c1], preferred_element_type=F32)
        up = jnp.dot(xb, wi_ref[:, hdim + c0:hdim + c1], preferred_element_type=F32)
        act = (_silu(gt) * up).astype(BF16)
        acc_ref[...] += jnp.dot(act, wo_ref[c0:c1, :], preferred_element_type=F32)
    y = acc_ref[...]
    if final_norm:
        ms2 = jnp.mean(y * y, axis=-1, keepdims=True)
        y = (y * lax.rsqrt(ms2 + RMS_EPS)) * gf_ref[...]
    o_ref[...] = y


def ffn(x, g, w_in, w_out, layer, g_final=None, pre=None, tm=512):
    m, d = x.shape
    tm = min(tm, m)
    assert m % tm == 0
    final_norm = g_final is not None
    in_specs = [pl.BlockSpec((tm, d), lambda i: (i, 0))]
    args = [x]
    if pre is not None:
        a, wa = pre
        in_specs += [pl.BlockSpec((tm, a.shape[1]), lambda i: (i, 0)), _const_spec(wa.shape)]
        args += [a, wa]
    layer_spec = lambda w: pl.BlockSpec((None,) + w.shape[1:], lambda i: (layer, 0, 0), pipeline_mode=pl.Buffered(1))
    in_specs += [_const_spec((1, d)), layer_spec(w_in), layer_spec(w_out)]
    args += [g.reshape(1, d).astype(F32), w_in, w_out]
    if final_norm:
        in_specs.append(_const_spec((1, d)))
        args.append(g_final.reshape(1, d).astype(F32))
    return pl.pallas_call(
        functools.partial(_ffn_kernel, final_norm=final_norm, pre=pre is not None),
        grid=(m // tm,),
        in_specs=in_specs,
        out_specs=pl.BlockSpec((tm, d), lambda i: (i, 0)),
        out_shape=jax.ShapeDtypeStruct((m, d), F32),
        scratch_shapes=[pltpu.VMEM((tm, d), F32)],
        compiler_params=_cparams(("parallel",)),
        name="ffn",
    )(*args)


def _ssd_kernel(*refs, L, valid, has_h0, nb):
    for bi in range(nb):
        _ssd_row(refs, bi, L=L, valid=valid, has_h0=has_h0)


def _ssd_row(refs, bi, *, L, valid, has_h0):
    if has_h0:
        (xbc_ref, z_ref, dtr_ref, tail_ref, h0_ref, cw_ref, cb_ref, dtb_ref, al_ref,
         dsk_ref, gn_ref, y_ref, hout_ref, xpad_all, hT_all) = refs
    else:
        (xbc_ref, z_ref, dtr_ref, tail_ref, cw_ref, cb_ref, dtb_ref, al_ref,
         dsk_ref, gn_ref, y_ref, hout_ref, xpad_all, hT_all) = refs
    xpad_scr = xpad_all.at[bi]
    hT_scr = hT_all.at[bi]
    c = pl.program_id(1)
    nc = pl.num_programs(1)

    @pl.when(c == 0)
    def _():
        xpad_scr[0:8, :] = tail_ref[bi]
        if has_h0:
            hT_scr[...] = h0_ref[bi].T
        else:
            hT_scr[...] = jnp.zeros(hT_scr.shape, F32)

    @pl.when(c > 0)
    def _():
        xpad_scr[0:8, :] = xpad_scr[L:L + 8, :]

    xpad_scr[8:8 + L, :] = xbc_ref[bi]
    acc = cb_ref[...]
    for j in range(CONV_K):
        acc = acc + xpad_scr[5 + j:5 + j + L, :] * cw_ref[j:j + 1, :]
    xc = _silu(acc)
    xs = xc[:, :SSM_WIDTH]
    gn_w = SSM_GROUPS * SSM_N
    bm = xc[:, SSM_WIDTH:SSM_WIDTH + gn_w]
    cm = xc[:, SSM_WIDTH + gn_w:]

    dt = _softplus(dtr_ref[bi] + dtb_ref[...])
    if valid < L:
        dt = jnp.where(_iota((L, LANES), 0) < valid, dt, 0.0)
    a128 = -jnp.exp(al_ref[...])

    expand = (_iota((LANES, SSM_WIDTH), 1) // SSM_P == _iota((LANES, SSM_WIDTH), 0)).astype(BF16)
    tril = (_iota((L, L), 0) >= _iota((L, L), 1))
    tril_b = tril.astype(BF16)

    dt_exp = _dot01_right(dt, expand)
    cs = _dot01_left(tril_b, dt * a128)
    cs_exp = _dot01_right(cs, expand)
    cs_t = cs.T

    last = cs_exp[L - 1:L, :]
    ecs = jnp.exp(cs_exp)
    wend = jnp.exp(last - cs_exp)
    dec = jnp.exp(last)
    xdt = xs * dt_exp
    xdt_b = xdt.astype(BF16)
    xw_b = (xdt * wend).astype(BF16)

    lane_lo = _iota((L, LANES), 1) < SSM_P
    gw = SSM_WIDTH // SSM_GROUPS
    y_parts = []
    for g in range(SSM_GROUPS):
        bg = bm[:, g * SSM_N:(g + 1) * SSM_N]
        cg_b = cm[:, g * SSM_N:(g + 1) * SSM_N].astype(BF16)
        cb = lax.dot_general(cg_b, bg.astype(BF16), NT_DIMS, preferred_element_type=F32)
        h_old = hT_scr[:, g * gw:(g + 1) * gw]
        y_state = jnp.dot(cg_b, h_old.astype(BF16), preferred_element_type=F32)
        pairs = []
        for i in range(gw // LANES):
            slab = g * (gw // LANES) + i
            xp = xdt_b[:, slab * LANES:(slab + 1) * LANES]
            halves = []
            for half in range(2):
                h = 2 * slab + half
                diff = cs[:, h:h + 1] - cs_t[h:h + 1, :]
                dm = jnp.exp(jnp.where(tril, diff, NEG_INF))
                sc = (cb * dm).astype(BF16)
                halves.append(jnp.dot(sc, xp, preferred_element_type=F32))
            pairs.append(jnp.where(lane_lo, halves[0], halves[1]))
        y_intra = jnp.concatenate(pairs, axis=1)
        y_parts.append(y_intra + y_state * ecs[:, g * gw:(g + 1) * gw])
        bg_t = bg.T.astype(BF16)
        upd = jnp.dot(bg_t, xw_b[:, g * gw:(g + 1) * gw], preferred_element_type=F32)
        hT_scr[:, g * gw:(g + 1) * gw] = h_old * dec[:, g * gw:(g + 1) * gw] + upd
    y = jnp.concatenate(y_parts, axis=1) + dsk_ref[...] * xs
    yg = y * _silu(z_ref[bi])
    ms = jnp.mean(yg * yg, axis=-1, keepdims=True)
    y_ref[bi] = ((yg * lax.rsqrt(ms + RMS_EPS)) * gn_ref[...]).astype(y_ref.dtype)

    @pl.when(c == nc - 1)
    def _():
        hout_ref[bi] = hT_scr[...].T


def ssd(xbc, z, dtr, tail, h0, conv_w, conv_b, dt_bias, a_log, d_skip, ssm_norm, L, valid, nb=1):
    b, t, _ = xbc.shape
    assert t % L == 0 and b % nb == 0
    nc = t // L
    has_h0 = h0 is not None
    pad16 = lambda v: jnp.pad(v.astype(F32), (0, LANES - v.shape[0])).reshape(1, LANES)
    rep = lambda v: jnp.repeat(v.astype(F32), SSM_P).reshape(1, SSM_WIDTH)
    row = lambda last: pl.BlockSpec((nb, L, last), lambda i, j: (i, j, 0))
    per_b = lambda s1, s2: pl.BlockSpec((nb, s1, s2), lambda i, j: (i, 0, 0))
    in_specs = [row(CONV_DIM), row(SSM_WIDTH), row(LANES), per_b(8, CONV_DIM)]
    args = [xbc, z, dtr, tail]
    if has_h0:
        in_specs.append(per_b(SSM_WIDTH, SSM_N))
        args.append(h0)
    consts = [conv_w.astype(F32), conv_b.reshape(1, CONV_DIM).astype(F32), pad16(dt_bias), pad16(a_log),
              rep(d_skip), ssm_norm.reshape(1, SSM_WIDTH).astype(F32)]
    in_specs += [_const_spec(cst.shape) for cst in consts]
    args += consts
    return pl.pallas_call(
        functools.partial(_ssd_kernel, L=L, valid=valid, has_h0=has_h0, nb=nb),
        grid=(b // nb, nc),
        in_specs=in_specs,
        out_specs=[row(SSM_WIDTH), per_b(SSM_WIDTH, SSM_N)],
        out_shape=[jax.ShapeDtypeStruct((b, t, SSM_WIDTH), BF16),
                   jax.ShapeDtypeStruct((b, SSM_WIDTH, SSM_N), F32)],
        scratch_shapes=[pltpu.VMEM((nb, L + 8, CONV_DIM), F32), pltpu.VMEM((nb, SSM_N, SSM_WIDTH), F32)],
        compiler_params=_cparams(("parallel", "arbitrary")),
        name="ssd",
    )(*args)


GROUP = N_HEADS // N_KV


def _head_perm():
    order = []
    for slab in range(N_HEADS // 2):
        j, g = slab // GROUP, slab % GROUP
        order += [(2 * j) * GROUP + g, (2 * j + 1) * GROUP + g]
    return np.asarray(order)


HEAD_PERM = _head_perm()
COL_PERM = (HEAD_PERM[:, None] * HEAD_DIM + np.arange(HEAD_DIM)[None, :]).reshape(-1)
SLABS_PER_STEP = 4
HEADS_PER_STEP = 2 * SLABS_PER_STEP


LOG2E = math.log2(math.e)
Q_SCALE = LOG2E * HEAD_DIM ** -0.5


def _masked_q(q_ref, qm_scr, first_head):
    tq = q_ref.shape[1]
    lane = _iota((tq, LANES), 1)
    lo = lane < HEAD_DIM
    for s in range(SLABS_PER_STEP):
        slab = q_ref[0, :, s * LANES:(s + 1) * LANES]
        zero = jnp.zeros_like(slab)
        qm_scr[2 * s, :, 0:LANES] = jnp.where(lo, slab, zero)
        qm_scr[2 * s + 1, :, 0:LANES] = jnp.where(lo, zero, slab)
    for h in range(HEADS_PER_STEP):
        mine = (lane < BIAS_PARTS * N_HEADS) & (lane % N_HEADS == first_head + h)
        qm_scr[h, :, LANES:2 * LANES] = jnp.where(mine, -1.0, 0.0).astype(qm_scr.dtype)


def _write_heads(o_ref, acc, inv_l):
    tq = o_ref.shape[1]
    lo = _iota((tq, LANES), 1) < HEAD_DIM
    for s in range(SLABS_PER_STEP):
        a = acc[2 * s] * inv_l[2 * s]
        b = acc[2 * s + 1] * inv_l[2 * s + 1]
        o_ref[0, :, s * LANES:(s + 1) * LANES] = jnp.where(lo, a, b).astype(o_ref.dtype)


def _swa_prompt_kernel(q_ref, kp_ref, kc_ref, vp_ref, vc_ref, sink_ref, o_ref):
    i = pl.program_id(1)
    w = WINDOW
    r = _iota((w, 2 * w), 0)
    col = _iota((w, 2 * w), 1)
    lim = jnp.where(i > 0, r, 2 * w)
    mask = ((col > lim) & (col < w)) | ((col >= w) & (col - w <= r))
    lo = _iota((w, LANES), 1) < HEAD_DIM
    slab_of = lambda ref, s: ref[0, :, s * LANES:(s + 1) * LANES]
    npair = N_KV // 2
    kcat = [jnp.concatenate([slab_of(kp_ref, j), slab_of(kc_ref, j)], axis=0) for j in range(npair)]
    vcat = [jnp.concatenate([slab_of(vp_ref, j), slab_of(vc_ref, j)], axis=0) for j in range(npair)]
    heads = range(N_HEADS)
    scores = []
    for h in heads:
        qs = slab_of(q_ref, h // 2)
        zero = jnp.zeros_like(qs)
        qm = jnp.where(lo, qs, zero) if h % 2 == 0 else jnp.where(lo, zero, qs)
        s = lax.dot_general(qm, kcat[h // HEADS_PER_STEP], NT_DIMS, preferred_element_type=F32)
        scores.append(jnp.where(mask, s, NEG_INF))
    probs, inv_l = [], []
    for h in heads:
        sink = sink_ref[h:h + 1, :] * LOG2E
        m = jnp.maximum(jnp.max(scores[h], axis=-1, keepdims=True), sink)
        p = jnp.exp2(scores[h] - jnp.tile(m, (1, 2 * w // LANES)))
        inv_l.append(1.0 / (jnp.sum(p, axis=-1, keepdims=True) + jnp.exp2(sink - m)))
        probs.append(p.astype(BF16))
    acc = [jnp.dot(probs[h], vcat[h // HEADS_PER_STEP], preferred_element_type=F32) * inv_l[h] for h in heads]
    for s in range(N_HEADS // 2):
        o_ref[0, :, s * LANES:(s + 1) * LANES] = jnp.where(lo, acc[2 * s], acc[2 * s + 1]).astype(o_ref.dtype)


def swa_prompt(q, kb, vb, sink_perm):
    b, t, _ = q.shape
    w = WINDOW
    nb = t // w
    kvw = N_KV * HEAD_DIM
    sink_arr = jnp.broadcast_to(sink_perm.astype(F32).reshape(N_HEADS, 1), (N_HEADS, LANES))
    qspec = pl.BlockSpec((1, w, N_HEADS * HEAD_DIM), lambda bi, i: (bi, i, 0))
    cur = pl.BlockSpec((1, w, kvw), lambda bi, i: (bi, i, 0))
    prev = pl.BlockSpec((1, w, kvw), lambda bi, i: (bi, jnp.maximum(i - 1, 0), 0))
    return pl.pallas_call(
        _swa_prompt_kernel,
        grid=(b, nb),
        in_specs=[qspec, prev, cur, prev, cur, _const_spec((N_HEADS, LANES))],
        out_specs=qspec,
        out_shape=jax.ShapeDtypeStruct(q.shape, BF16),
        compiler_params=_cparams(("parallel", "arbitrary")),
        name="swa_prompt",
    )(q, kb, kb, vb, vb, sink_arr)


BIAS_PARTS = 3


def _logf_scan_kernel(lf_ref, parts_ref, lft_ref, carry_scr):
    c = pl.program_id(1)

    @pl.when(c == 0)
    def _():
        carry_scr[...] = jnp.zeros(carry_scr.shape, F32)

    L = lf_ref.shape[1]
    lf = lf_ref[0]
    tril_b = (_iota((L, L), 0) >= _iota((L, L), 1)).astype(BF16)
    cs = _dot01_left(tril_b, lf) + carry_scr[...]
    carry_scr[...] = cs[L - 1:L, :]
    src = _iota((LANES, LANES), 0)
    dst = _iota((LANES, LANES), 1)
    out = jnp.zeros((L, LANES), F32)
    for k, piece in enumerate(_split3(cs * LOG2E)):
        sel = ((src < N_HEADS) & (dst == src + k * N_HEADS)).astype(BF16)
        out = out + jnp.dot(piece, sel, preferred_element_type=F32)
    parts_ref[0] = out.astype(BF16)
    lft_ref[0] = lf.T[N_HEADS:2 * N_HEADS, :]


def logf_scan(lf):
    b, t, _ = lf.shape
    L = min(512, t)
    return pl.pallas_call(
        _logf_scan_kernel,
        grid=(b, t // L),
        in_specs=[pl.BlockSpec((1, L, LANES), lambda i, j: (i, j, 0))],
        out_specs=[pl.BlockSpec((1, L, LANES), lambda i, j: (i, j, 0)),
                   pl.BlockSpec((1, N_HEADS, L), lambda i, j: (i, 0, j))],
        out_shape=[jax.ShapeDtypeStruct((b, t, LANES), BF16), jax.ShapeDtypeStruct((b, N_HEADS, t), F32)],
        scratch_shapes=[pltpu.VMEM((1, LANES), F32)],
        compiler_params=_cparams(("parallel", "arbitrary")),
        name="logf_scan",
    )(lf)


def _fox_prompt_kernel(qi_ref, ki_ref, q_ref, k_ref, v_ref, c_ref, o_ref, qm_scr, m_scr, acc_scr, *, tq, tk):
    step = pl.program_id(2)
    qi = qi_ref[step]
    ki = ki_ref[step]
    last_k = (qi * tq + tq - 1) // tk
    heads = range(HEADS_PER_STEP)

    @pl.when(ki == 0)
    def _():
        _masked_q(q_ref, qm_scr, pl.program_id(1) * HEADS_PER_STEP)
        m_scr[...] = jnp.full(m_scr.shape, NEG_INF, F32)
        acc_scr[...] = jnp.zeros(acc_scr.shape, F32)

    def body(r0, r1, nk, masked):
        rows = slice(r0, r1)
        kb = jnp.concatenate([k_ref[0, 0:nk, :], c_ref[0, 0:nk, :]], axis=1)
        vb = v_ref[0, 0:nk, :]
        vb_ext = jnp.concatenate([vb, jnp.ones_like(vb)], axis=1)
        if masked:
            row = qi * tq + r0 + _iota((r1 - r0, nk), 0)
            col = ki * tk + _iota((r1 - r0, nk), 1)
            keep = col <= row
        scores = []
        for h in heads:
            s = lax.dot_general(qm_scr[h, rows, :], kb, NT_DIMS, preferred_element_type=F32)
            scores.append(jnp.where(keep, s, NEG_INF) if masked else s)
        probs, alphas = [], []
        for h in heads:
            m_prev = m_scr[h, rows, :]
            m_new = jnp.maximum(m_prev, jnp.max(scores[h], axis=-1, keepdims=True))
            alphas.append(jnp.exp2(m_prev - m_new))
            probs.append(jnp.exp2(scores[h] - jnp.tile(m_new, (1, nk // LANES))).astype(BF16))
            m_scr[h, rows, :] = m_new
        for h in heads:
            acc_scr[h, rows, :] = (jnp.tile(alphas[h], (1, 2)) * acc_scr[h, rows, :]
                                   + jnp.dot(probs[h], vb_ext, preferred_element_type=F32))

    on_diagonal = (ki + 1) * tk - 1 > qi * tq

    @pl.when(on_diagonal)
    def _():
        if tq == tk and tq % (2 * LANES) == 0:
            body(0, tq // 2, tk // 2, True)
            body(tq // 2, tq, tk, True)
        else:
            body(0, tq, tk, True)

    @pl.when(jnp.logical_not(on_diagonal))
    def _():
        body(0, tq, tk, False)

    @pl.when(ki == last_k)
    def _():
        _write_heads(o_ref, [acc_scr[h, :, 0:LANES] for h in heads],
                     [1.0 / acc_scr[h, :, LANES:2 * LANES] for h in heads])


def fox_prompt(q, kb, vb, bias_parts, tq=512, tk=512):
    b, t, _ = q.shape
    tq, tk = min(tq, t), min(tk, t)
    npair = N_KV // 2
    qi_l, ki_l = [], []
    for qi in range(t // tq):
        for ki in range((qi * tq + tq - 1) // tk + 1):
            qi_l.append(qi)
            ki_l.append(ki)
    qi_arr = jnp.asarray(qi_l, jnp.int32)
    ki_arr = jnp.asarray(ki_l, jnp.int32)
    qspec = pl.BlockSpec((1, tq, SLABS_PER_STEP * LANES), lambda bi, j, s, qi, ki: (bi, qi[s], j))
    kspec = pl.BlockSpec((1, tk, LANES), lambda bi, j, s, qi, ki: (bi, ki[s], j))
    cspec = pl.BlockSpec((1, tk, LANES), lambda bi, j, s, qi, ki: (bi, ki[s], 0))
    grid_spec = pltpu.PrefetchScalarGridSpec(
        num_scalar_prefetch=2,
        grid=(b, npair, len(qi_l)),
        in_specs=[qspec, kspec, kspec, cspec],
        out_specs=qspec,
        scratch_shapes=[pltpu.VMEM((HEADS_PER_STEP, tq, 2 * LANES), BF16),
                        pltpu.VMEM((HEADS_PER_STEP, tq, LANES), F32),
                        pltpu.VMEM((HEADS_PER_STEP, tq, 2 * LANES), F32)],
    )
    return pl.pallas_call(
        functools.partial(_fox_prompt_kernel, tq=tq, tk=tk),
        grid_spec=grid_spec,
        out_shape=jax.ShapeDtypeStruct(q.shape, BF16),
        compiler_params=_cparams(("parallel", "parallel", "arbitrary")),
        name="fox_prompt",
    )(qi_arr, ki_arr, q, kb, vb, bias_parts)


def _xattn_kernel(q_ref, k_ref, v_ref, o_ref, *, nb, mem, interleaved):
    scale = LOG2E * XA_DIM ** -0.5
    pairs = [(bi, h) for bi in range(nb) for h in range(XA_HEADS)]
    lanes = lambda h: slice(h * XA_DIM, (h + 1) * XA_DIM)

    def kv_head(ref, bi, h):
        if interleaved:
            return ref[bi, pl.ds(h, mem, stride=XA_HEADS), :].astype(BF16)
        return ref[bi, :, lanes(h)].astype(BF16)

    scores = [lax.dot_general(q_ref[bi, :, lanes(h)], kv_head(k_ref, bi, h), NT_DIMS,
                              preferred_element_type=F32) * scale for bi, h in pairs]
    probs, inv_l = [], []
    for s in scores:
        p = jnp.exp2(s - jnp.max(s, axis=-1, keepdims=True))
        inv_l.append(1.0 / jnp.sum(p, axis=-1, keepdims=True))
        probs.append(p.astype(BF16))
    for (bi, h), p, il in zip(pairs, probs, inv_l):
        o = jnp.dot(p, kv_head(v_ref, bi, h), preferred_element_type=F32) * il
        o_ref[bi, :, lanes(h)] = o.astype(o_ref.dtype)


def xattn(q, mk, mv, tq, nb, kv_first=0):
    b, t, _ = q.shape
    interleaved = mk.shape[2] == XA_DIM
    mem = mk.shape[1] // XA_HEADS if interleaved else mk.shape[1]
    assert kv_first % nb == 0
    off = kv_first // nb
    qspec = pl.BlockSpec((nb, tq, XA_WIDTH), lambda i, j: (i, j, 0))
    mspec = pl.BlockSpec((nb,) + mk.shape[1:], lambda i, j: (i + off, 0, 0))
    return pl.pallas_call(
        functools.partial(_xattn_kernel, nb=nb, mem=mem, interleaved=interleaved),
        grid=(b // nb, t // tq),
        in_specs=[qspec, mspec, mspec],
        out_specs=qspec,
        out_shape=jax.ShapeDtypeStruct(q.shape, BF16),
        compiler_params=_cparams(("parallel", "arbitrary")),
        name="xattn",
    )(q, mk, mv)


def _block_diag_q(q, nb, t):
    q4 = q.reshape(nb, t * N_HEADS, HEAD_DIM)
    qt = jnp.tile(q4, (1, 1, N_KV))
    row_kv = (np.arange(t * N_HEADS) % N_HEADS) // GROUP
    keep = jnp.asarray(row_kv[:, None] == (np.arange(N_KV * HEAD_DIM) // HEAD_DIM)[None, :])
    return jnp.where(keep[None], qt * jnp.asarray(HEAD_DIM ** -0.5, BF16), jnp.zeros((), BF16))


def _diag_extract(acc, rows):
    row_kv = (_iota((rows, HEAD_DIM), 0) % N_HEADS) // GROUP
    out = jnp.zeros((rows, HEAD_DIM), F32)
    for kv in range(N_KV):
        out = out + jnp.where(row_kv == kv, acc[:, kv * HEAD_DIM:(kv + 1) * HEAD_DIM], 0.0)
    return out


NEW_PAD = 16


def _swa_sample_kernel(q_ref, ck_ref, cv_ref, kn_ref, vn_ref, sink_ref, o_ref, ok_ref, ov_ref, *, nb, t):
    w = WINDOW
    rows = t * N_HEADS
    tok_c = _iota((rows, w), 0) // N_HEADS
    keep_c = _iota((rows, w), 1) > tok_c
    keep_n = _iota((rows, NEW_PAD), 1) <= _iota((rows, NEW_PAD), 0) // N_HEADS
    sink = sink_ref[:, 0:1]
    place = (_iota((w, NEW_PAD), 0) == _iota((w, NEW_PAD), 1) + (w - t)).astype(BF16)
    old_slots = _iota((N_KV * HEAD_DIM, w), 1) < w - t
    for bi in range(nb):
        ck = ck_ref[bi]
        cv = cv_ref[bi]
        kn = kn_ref[bi]
        vn = vn_ref[bi]
        q = q_ref[bi]
        s_c = jnp.where(keep_c, jnp.dot(q, ck.astype(BF16), preferred_element_type=F32), NEG_INF)
        s_n = jnp.where(keep_n, lax.dot_general(q, kn.astype(BF16), NT_DIMS, preferred_element_type=F32), NEG_INF)
        m = jnp.maximum(jnp.maximum(jnp.max(s_c, axis=-1, keepdims=True), jnp.max(s_n, axis=-1, keepdims=True)), sink)
        p_c = jnp.exp(s_c - m)
        p_n = jnp.exp(s_n - m)
        l = jnp.sum(p_c, axis=-1, keepdims=True) + jnp.sum(p_n, axis=-1, keepdims=True) + jnp.exp(sink - m)
        acc = (lax.dot_general(p_c.astype(BF16), cv.astype(BF16), NT_DIMS, preferred_element_type=F32)
               + jnp.dot(p_n.astype(BF16), vn.astype(BF16), preferred_element_type=F32))
        o_ref[bi] = (_diag_extract(acc, rows) * (1.0 / l)).astype(o_ref.dtype)
        ok_ref[bi] = jnp.where(old_slots, pltpu.roll(ck, w - t, 1), _dot01_left(place, kn).T)
        ov_ref[bi] = jnp.where(old_slots, pltpu.roll(cv, w - t, 1), _dot01_left(place, vn).T)


def swa_sample(qbd, ck_t, cv_t, kn, vn, sink, t, nb=8):
    b = qbd.shape[0]
    w = WINDOW
    rows = t * N_HEADS
    sink_rows = jnp.broadcast_to(jnp.tile(sink.astype(F32), t).reshape(rows, 1), (rows, LANES))
    bspec = lambda s1, s2: pl.BlockSpec((nb, s1, s2), lambda i: (i, 0, 0))
    kvw = N_KV * HEAD_DIM
    return pl.pallas_call(
        functools.partial(_swa_sample_kernel, nb=nb, t=t),
        grid=(b // nb,),
        in_specs=[bspec(rows, kvw), bspec(kvw, w), bspec(kvw, w), bspec(NEW_PAD, kvw), bspec(NEW_PAD, kvw),
                  _const_spec((rows, LANES))],
        out_specs=[bspec(rows, HEAD_DIM), bspec(kvw, w), bspec(kvw, w)],
        out_shape=[jax.ShapeDtypeStruct((b, rows, HEAD_DIM), BF16),
                   jax.ShapeDtypeStruct((b, kvw, w), F32), jax.ShapeDtypeStruct((b, kvw, w), F32)],
        compiler_params=_cparams(("parallel",)),
        name="swa_sample",
    )(qbd, ck_t, cv_t, kn, vn, sink_rows)


FOX_PAGES_PER_STEP = 64


def _fox_sample_kernel(pt_ref, q_ref, kn_ref, vn_ref, lfn_ref, kc_hbm, vc_hbm, lf_hbm, o_ref,
                       kbuf, vbuf, lbuf, sem, carry_scr, m_scr, l_scr, acc_scr,
                       *, t, n_pages, pg):
    b = pl.program_id(0)
    c = pl.program_id(1)
    nch = pl.num_programs(1)
    step = b * nch + c
    total = pl.num_programs(0) * nch
    rows = t * N_HEADS
    nh = N_HEADS

    def copies(stp, slot):
        base = (stp // nch) * n_pages + (stp % nch) * pg
        out = []
        for p in range(pg):
            page = pt_ref[base + p]
            out.append(pltpu.make_async_copy(kc_hbm.at[page], kbuf.at[slot, :, pl.ds(p * PAGE, PAGE)], sem.at[slot, 0]))
            out.append(pltpu.make_async_copy(vc_hbm.at[page], vbuf.at[slot, :, pl.ds(p * PAGE, PAGE)], sem.at[slot, 1]))
            out.append(pltpu.make_async_copy(lf_hbm.at[page], lbuf.at[slot, pl.ds(p * nh, nh)], sem.at[slot, 2]))
        return out

    slot = step % 2

    @pl.when(step == 0)
    def _():
        for cp in copies(step, slot):
            cp.start()

    @pl.when(step + 1 < total)
    def _():
        for cp in copies(step + 1, 1 - slot):
            cp.start()

    @pl.when(c == 0)
    def _():
        carry_scr[...] = jnp.zeros(carry_scr.shape, F32)
        m_scr[...] = jnp.full(m_scr.shape, NEG_INF, F32)
        l_scr[...] = jnp.zeros(l_scr.shape, F32)
        acc_scr[...] = jnp.zeros(acc_scr.shape, F32)

    for cp in copies(step, slot):
        cp.wait()

    q = q_ref[0]
    upper_b = (_iota((PAGE, PAGE), 0) <= _iota((PAGE, PAGE), 1)).astype(BF16)

    def online(s, pv):
        m_prev = m_scr[...]
        m_new = jnp.maximum(m_prev, jnp.max(s, axis=-1, keepdims=True))
        alpha = jnp.exp(m_prev - m_new)
        p = jnp.exp(s - m_new)
        l_scr[...] = alpha * l_scr[...] + jnp.sum(p, axis=-1, keepdims=True)
        acc_scr[...] = alpha * acc_scr[...] + pv(p.astype(BF16))
        m_scr[...] = m_new

    lf = lbuf[slot]
    cl = _dot01_right(lf, upper_b)
    tot = jnp.broadcast_to(cl[:, PAGE - 1:PAGE], cl.shape)
    r = _iota((pg * nh, pg * nh), 0)
    cc = _iota((pg * nh, pg * nh), 1)
    earlier_b = ((r % nh == cc % nh) & (cc // nh < r // nh)).astype(BF16)
    cum = cl + _dot01_left(earlier_b, tot) + jnp.concatenate([carry_scr[...]] * pg, axis=0)
    carry_new = jnp.broadcast_to(cum[(pg - 1) * nh:pg * nh, PAGE - 1:PAGE], (nh, LANES))
    carry_scr[...] = carry_new
    bias16 = jnp.concatenate([cum[p * nh:(p + 1) * nh, :] for p in range(pg)], axis=1)
    bias = jnp.concatenate([bias16] * t, axis=0)

    kc = kbuf[slot].astype(BF16)
    vc = vbuf[slot].astype(BF16)
    s = jnp.dot(q, kc, preferred_element_type=F32) - bias
    online(s, lambda p: lax.dot_general(p, vc, NT_DIMS, preferred_element_type=F32))

    @pl.when(c == nch - 1)
    def _():
        zpad = jnp.zeros((PAGE - NEW_PAD, N_KV * HEAD_DIM), BF16)
        knew = jnp.concatenate([kn_ref[0].astype(BF16), zpad], axis=0)
        vnew = jnp.concatenate([vn_ref[0].astype(BF16), zpad], axis=0)
        cum_n = _dot01_right(lfn_ref[0], upper_b) + carry_new
        sn = lax.dot_general(q, knew, NT_DIMS, preferred_element_type=F32)
        sn = sn - jnp.concatenate([cum_n] * t, axis=0)
        tok = _iota((rows, PAGE), 0) // nh
        sn = jnp.where(_iota((rows, PAGE), 1) <= tok, sn, NEG_INF)
        online(sn, lambda p: jnp.dot(p, vnew, preferred_element_type=F32))
        o_ref[0] = (_diag_extract(acc_scr[...], rows) * (1.0 / l_scr[...])).astype(o_ref.dtype)


def fox_sample(page_table, qbd, kn, vn, lfn_t, kcache, vcache, lf_t_cache, t, pg=FOX_PAGES_PER_STEP):
    b, n_pages = page_table.shape
    pg = min(pg, n_pages)
    assert n_pages % pg == 0
    rows = t * N_HEADS
    kvw = N_KV * HEAD_DIM
    bspec = lambda s1, s2: pl.BlockSpec((1, s1, s2), lambda i, j, pt: (i, 0, 0))
    any_spec = pl.BlockSpec(memory_space=pl.ANY)
    grid_spec = pltpu.PrefetchScalarGridSpec(
        num_scalar_prefetch=1,
        grid=(b, n_pages // pg),
        in_specs=[bspec(rows, kvw), bspec(NEW_PAD, kvw), bspec(NEW_PAD, kvw), bspec(N_HEADS, LANES),
                  any_spec, any_spec, any_spec],
        out_specs=bspec(rows, HEAD_DIM),
        scratch_shapes=[pltpu.VMEM((2, kvw, pg * PAGE), F32), pltpu.VMEM((2, kvw, pg * PAGE), F32),
                        pltpu.VMEM((2, pg * N_HEADS, LANES), F32), pltpu.SemaphoreType.DMA((2, 3)),
                        pltpu.VMEM((N_HEADS, LANES), F32),
                        pltpu.VMEM((rows, 1), F32), pltpu.VMEM((rows, 1), F32), pltpu.VMEM((rows, kvw), F32)],
    )
    return pl.pallas_call(
        functools.partial(_fox_sample_kernel, t=t, n_pages=n_pages, pg=pg),
        grid_spec=grid_spec,
        out_shape=jax.ShapeDtypeStruct((b, rows, HEAD_DIM), BF16),
        compiler_params=_cparams(("arbitrary", "arbitrary")),
        name="fox_sample",
    )(page_table.reshape(-1), qbd, kn, vn, lfn_t, kcache, vcache, lf_t_cache)


def _pad_cols(w, width):
    return jnp.pad(w, ((0, 0), (0, width - w.shape[1])))


def kernel(x_prompt, x_sample, state_ssm, state_conv, cache_swa_k, cache_swa_v, cache_fox_k, cache_fox_v,
           cache_fox_logf, cache_mem_k, cache_mem_v, page_table, mem_prompt, norm_mix, norm_xa, norm_mem,
           norm_ffn, w_in_even, conv_w, conv_b, dt_bias, a_log, d_skip, ssm_norm, swa_sink, w_out_even,
           w_in_odd, fox_fb, w_out_odd, w_xq, w_xk, w_xv, w_xo, w_ffn_in, w_ffn_out, norm_final):
    bp, seq, d = x_prompt.shape
    bs, ts, _ = x_sample.shape
    depth = norm_mix.shape[0]
    mem_len = mem_prompt.shape[1]
    kvw = N_KV * HEAD_DIM
    qw = N_HEADS * HEAD_DIM
    xp = x_prompt.reshape(bp * seq, d)
    xs = x_sample.reshape(bs * ts, d)
    mem2 = mem_prompt.reshape(bp * mem_len, d)
    w_ffn_in_b = w_ffn_in.astype(BF16)
    w_ffn_out_b = w_ffn_out.astype(BF16)
    outs = {k: [] for k in ("p_ssm", "p_conv", "p_swk", "p_swv", "p_fk", "p_fv", "p_fl", "p_mk", "p_mv",
                            "s_ssm", "s_conv", "s_swk", "s_swv", "s_fk", "s_fv", "s_fl")}

    for l in range(depth):
        li = l // 2
        if l % 2 == 0:
            w = w_in_even[li]
            o_z, o_xbc, o_dt = 0, SSM_WIDTH, SSM_WIDTH + CONV_DIM
            o_q = o_dt + SSM_HEADS
            o_k, o_v = o_q + qw, o_q + qw + kvw
            wz, wxbc, wdt = w[:, o_z:o_xbc], w[:, o_xbc:o_dt], _pad_cols(w[:, o_dt:o_q], LANES)
            wq, wk, wv = w[:, o_q:o_k], w[:, o_k:o_v], w[:, o_v:o_v + kvw]
            c_xbc, c_q, c_k, c_v, c_dt = SSM_WIDTH, SSM_WIDTH + CONV_DIM, SSM_WIDTH + CONV_DIM + qw, \
                SSM_WIDTH + CONV_DIM + qw + kvw, SSM_WIDTH + CONV_DIM + qw + 2 * kvw
            n_tot = c_dt + LANES
            wo = w_out_even[li]
            wo_ssm = wo[:SSM_WIDTH].astype(BF16)
            wo_att = wo[SSM_WIDTH:]
            ssm_params = (conv_w[li], conv_b[li], dt_bias[li], a_log[li], d_skip[li], ssm_norm[li])

            w_p = jnp.concatenate([wz, wxbc, wq[:, COL_PERM], wk, wv, wdt], axis=1).astype(BF16)
            segs_p = [(0, c_xbc, (F32,), None), (c_xbc, c_q, (F32,), None), (c_q, c_k, (BF16,), Q_SCALE),
                      (c_k, c_v, (F32, BF16), None), (c_v, c_dt, (F32, BF16), None), (c_dt, n_tot, (F32,), None)]
            z, xbc, q, k, kb, v, vb, dtr = norm_proj(xp, norm_mix[l], w_p, segs_p)
            xbc3 = xbc.reshape(bp, seq, CONV_DIM)
            y_ssm, h = ssd(xbc3, z.reshape(bp, seq, SSM_WIDTH), dtr.reshape(bp, seq, LANES),
                           jnp.zeros((bp, 8, CONV_DIM), F32), None, *ssm_params, L=SSM_CHUNK, valid=SSM_CHUNK,
                           nb=math.gcd(bp, 4))
            y_att = swa_prompt(q.reshape(bp, seq, qw), kb.reshape(bp, seq, kvw), vb.reshape(bp, seq, kvw),
                               swa_sink[li][HEAD_PERM])
            mix_p = ([y_ssm.reshape(bp * seq, SSM_WIDTH), y_att.reshape(bp * seq, qw)],
                     [wo_ssm, wo_att[COL_PERM].astype(BF16)])
            last_w = lambda a: a.reshape(bp, seq, kvw)[:, seq - WINDOW:].reshape(bp, WINDOW, N_KV, HEAD_DIM)
            outs["p_ssm"].append(h.reshape(bp, SSM_HEADS, SSM_P, SSM_N))
            outs["p_conv"].append(xbc3[:, seq - (CONV_K - 1):])
            outs["p_swk"].append(last_w(k))
            outs["p_swv"].append(last_w(v))

            w_s = jnp.concatenate([wz, wxbc, wq, wk, wv, wdt], axis=1).astype(BF16)
            segs_s = [(0, c_xbc, (F32,), None), (c_xbc, c_q, (F32,), None), (c_q, c_k, (BF16,), None),
                      (c_k, c_v, (F32,), None), (c_v, c_dt, (F32,), None), (c_dt, n_tot, (F32,), None)]
            z, xbc, q, k, v, dtr = norm_proj(xs, norm_mix[l], w_s, segs_s)
            ls = NEW_PAD
            pad_t = lambda a: jnp.pad(a.reshape(bs, ts, a.shape[-1]), ((0, 0), (0, ls - ts), (0, 0)))
            xbc3 = xbc.reshape(bs, ts, CONV_DIM)
            tail = jnp.pad(state_conv[li].astype(F32), ((0, 0), (8 - (CONV_K - 1), 0), (0, 0)))
            y_ssm, h = ssd(pad_t(xbc), pad_t(z), pad_t(dtr), tail, state_ssm[li].reshape(bs, SSM_WIDTH, SSM_N),
                           *ssm_params, L=ls, valid=ts, nb=math.gcd(bs, 8))
            y_ssm = y_ssm[:, :ts].reshape(bs * ts, SSM_WIDTH)
            kn, vn = pad_t(k), pad_t(v)
            win_t = lambda c: jnp.transpose(c[li].astype(F32), (0, 2, 3, 1)).reshape(bs, kvw, WINDOW)
            o_att, new_k, new_v = swa_sample(_block_diag_q(q, bs, ts), win_t(cache_swa_k), win_t(cache_swa_v),
                                             kn, vn, swa_sink[li], ts)
            win_back = lambda c: jnp.transpose(c.reshape(bs, N_KV, HEAD_DIM, WINDOW), (0, 3, 1, 2))
            mix_s = ([y_ssm, o_att.reshape(bs * ts, qw)], [wo_ssm, wo_att.astype(BF16)])
            outs["s_ssm"].append(h.reshape(bs, SSM_HEADS, SSM_P, SSM_N))
            outs["s_conv"].append(jnp.concatenate([state_conv[li].astype(F32), xbc3], axis=1)[:, -(CONV_K - 1):])
            outs["s_swk"].append(win_back(new_k))
            outs["s_swv"].append(win_back(new_v))
        else:
            w = w_in_odd[li]
            wq, wk, wv, wf = w[:, :qw], w[:, qw:qw + kvw], w[:, qw + kvw:qw + 2 * kvw], w[:, qw + 2 * kvw:]
            c_k, c_v, c_f = qw, qw + kvw, qw + 2 * kvw
            n_tot = c_f + LANES
            fb = fox_fb[li].astype(F32)

            wf_p = _pad_cols(jnp.concatenate([wf[:, HEAD_PERM], wf], axis=1), LANES)
            fb_p = jnp.pad(jnp.concatenate([fb[HEAD_PERM], fb]), (0, LANES - 2 * N_HEADS)).reshape(1, LANES)
            w_p = jnp.concatenate([wq[:, COL_PERM], wk, wv, wf_p], axis=1).astype(BF16)
            segs_p = [(0, c_k, (BF16,), Q_SCALE), (c_k, c_v, ("T", BF16), None), (c_v, c_f, ("T", BF16), None),
                      (c_f, n_tot, (F32,), "logsig")]
            q, k_t, kb, v_t, vb, lf = norm_proj(xp, norm_mix[l], w_p, segs_p, aux=fb_p, rows_per_batch=seq)
            lf3 = lf.reshape(bp, seq, LANES)
            bias_parts, lf_t = logf_scan(lf3)
            y = fox_prompt(q.reshape(bp, seq, qw), kb.reshape(bp, seq, kvw), vb.reshape(bp, seq, kvw), bias_parts)
            mix_p = ([y.reshape(bp * seq, qw)], [w_out_odd[li][COL_PERM].astype(BF16)])
            untr = lambda a: jnp.transpose(a.reshape(bp, N_KV, HEAD_DIM, seq), (0, 3, 1, 2))
            outs["p_fk"].append(untr(k_t))
            outs["p_fv"].append(untr(v_t))
            outs["p_fl"].append(jnp.swapaxes(lf_t, 1, 2))

            w_s = jnp.concatenate([wq, wk, wv, _pad_cols(wf, LANES)], axis=1).astype(BF16)
            fb_s = jnp.pad(fb, (0, LANES - N_HEADS)).reshape(1, LANES)
            segs_s = [(0, c_k, (BF16,), None), (c_k, c_v, (F32,), None), (c_v, c_f, (F32,), None),
                      (c_f, n_tot, (F32,), "logsig")]
            q, k, v, lf = norm_proj(xs, norm_mix[l], w_s, segs_s, aux=fb_s)
            lf_new = lf[:, :N_HEADS].reshape(bs, ts, N_HEADS)
            pad_t = lambda a: jnp.pad(a.reshape(bs, ts, a.shape[-1]), ((0, 0), (0, NEW_PAD - ts), (0, 0)))
            lfn_t = jnp.pad(jnp.swapaxes(lf_new, 1, 2), ((0, 0), (0, 0), (0, LANES - ts)))
            pool = cache_fox_k.shape[1]
            page_t = lambda c: jnp.transpose(c[li].astype(F32), (0, 2, 3, 1)).reshape(pool, kvw, PAGE)
            y = fox_sample(page_table, _block_diag_q(q, bs, ts), pad_t(k), pad_t(v), lfn_t,
                           page_t(cache_fox_k), page_t(cache_fox_v),
                           jnp.swapaxes(cache_fox_logf[li].astype(F32), 1, 2), ts)
            mix_s = ([y.reshape(bs * ts, qw)], [w_out_odd[li].astype(BF16)])
            outs["s_fk"].append(k.reshape(bs, ts, N_KV, HEAD_DIM))
            outs["s_fv"].append(v.reshape(bs, ts, N_KV, HEAD_DIM))
            outs["s_fl"].append(lf_new)

        wkv = jnp.concatenate([w_xk[l], w_xv[l]], axis=1).astype(BF16)
        mk, mv = norm_proj(mem2, norm_mem[l], wkv, [(0, XA_WIDTH, (F32,), None), (XA_WIDTH, 2 * XA_WIDTH, (F32,), None)])
        wq_b = w_xq[l].astype(BF16)
        wo_b = w_xo[l].astype(BF16)
        xp, qx = matmul_residual(*mix_p, xp, proj=(norm_xa[l], wq_b))
        ox_p = xattn(qx.reshape(bp, seq, XA_WIDTH), mk.reshape(bp, mem_len, XA_WIDTH),
                     mv.reshape(bp, mem_len, XA_WIDTH), tq=512, nb=1).reshape(bp * seq, XA_WIDTH)
        xs, qx = matmul_residual(*mix_s, xs, proj=(norm_xa[l], wq_b))
        qx = jnp.pad(qx.reshape(bs, ts, XA_WIDTH), ((0, 0), (0, NEW_PAD - ts), (0, 0)))
        all_layers = lambda c: c.astype(F32).reshape(depth * bs, mem_len * XA_HEADS, XA_DIM)
        ox_s = xattn(qx, all_layers(cache_mem_k), all_layers(cache_mem_v), tq=NEW_PAD, nb=4, kv_first=l * bs)
        ox_s = ox_s[:, :ts].reshape(bs * ts, XA_WIDTH)
        outs["p_mk"].append(mk.reshape(bp, mem_len, XA_HEADS, XA_DIM))
        outs["p_mv"].append(mv.reshape(bp, mem_len, XA_HEADS, XA_DIM))

        gf = norm_final if l == depth - 1 else None
        xp = ffn(xp, norm_ffn[l], w_ffn_in_b, w_ffn_out_b, l, g_final=gf, pre=(ox_p, wo_b))
        xs = ffn(xs, norm_ffn[l], w_ffn_in_b, w_ffn_out_b, l, g_final=gf, pre=(ox_s, wo_b))

    st = lambda key: jnp.stack(outs[key])
    return (xp.reshape(bp, seq, d), xs.reshape(bs, ts, d),
            st("p_ssm"), st("p_conv"), st("p_swk"), st("p_swv"), st("p_fk"), st("p_fv"), st("p_fl"),
            st("p_mk"), st("p_mv"), st("s_ssm"), st("s_conv"), st("s_swk"), st("s_swv"),
            st("s_fk"), st("s_fv"), st("s_fl"))
```

```python
import functools
import math

import numpy as np
import jax
import jax.numpy as jnp
from jax import lax
from jax.experimental import pallas as pl
from jax.experimental.pallas import tpu as pltpu

F32 = jnp.float32
BF16 = jnp.bfloat16

HEAD_DIM = 64
N_HEADS = 16
N_KV = 4
SSM_HEADS = 16
SSM_P = 64
SSM_WIDTH = 1024
SSM_N = 128
SSM_GROUPS = 2
SSM_CHUNK = 128
CONV_K = 4
CONV_DIM = 1536
WINDOW = 128
PAGE = 128
XA_HEADS = 4
XA_DIM = 128
XA_WIDTH = 512
RMS_EPS = 1e-6
NEG_INF = -1e30

V7X_VMEM_BYTES = 64 * 1024 * 1024
VMEM_LIMIT = V7X_VMEM_BYTES - 8 * 1024 * 1024
LANES = 128

NT_DIMS = (((1,), (1,)), ((), ()))


def _cparams(sem):
    return pltpu.CompilerParams(dimension_semantics=sem, vmem_limit_bytes=VMEM_LIMIT)


def _const_spec(shape):
    nd = len(shape)
    return pl.BlockSpec(shape, lambda *_: (0,) * nd, pipeline_mode=pl.Buffered(1))


def _softplus(x):
    return jnp.maximum(x, 0.0) + jnp.log1p(jnp.exp(-jnp.abs(x)))


def _silu(x):
    return x * (1.0 / (1.0 + jnp.exp(-x)))


def _split3(x):
    hi = x.astype(BF16)
    r1 = x - hi.astype(F32)
    mid = r1.astype(BF16)
    lo = (r1 - mid.astype(F32)).astype(BF16)
    return hi, mid, lo


def _dot01_left(m01, x):
    hi, mid, lo = _split3(x)
    d = lambda b: jnp.dot(m01, b, preferred_element_type=F32)
    return (d(lo) + d(mid)) + d(hi)


def _dot01_right(x, m01):
    hi, mid, lo = _split3(x)
    d = lambda a: jnp.dot(a, m01, preferred_element_type=F32)
    return (d(lo) + d(mid)) + d(hi)


def _iota(shape, dim):
    return lax.broadcasted_iota(jnp.int32, shape, dim)


def _norm_proj_kernel(*refs, segs, has_aux):
    x_ref, g_ref, w_ref = refs[:3]
    pos = 3
    aux_ref = None
    if has_aux:
        aux_ref = refs[3]
        pos = 4
    out_refs = refs[pos:]
    x = x_ref[...]
    ms = jnp.mean(x * x, axis=-1, keepdims=True)
    xn = (x * lax.rsqrt(ms + RMS_EPS)) * g_ref[...]
    xb = xn.astype(BF16)
    oi = 0
    for (c0, c1, dtypes, act) in segs:
        y = jnp.dot(xb, w_ref[:, c0:c1], preferred_element_type=F32)
        if act == "logsig":
            y = -_softplus(-(y + aux_ref[...]))
        elif act is not None:
            y = y * act
        for dt in dtypes:
            if dt == "T":
                out_refs[oi][0] = y.T
            else:
                out_refs[oi][...] = y.astype(dt)
            oi += 1


def norm_proj(x, g, w, segs, aux=None, tm=512, rows_per_batch=None):
    m, d = x.shape
    tm = min(tm, m)
    assert m % tm == 0
    n = w.shape[1]
    tiles_per_b = (rows_per_batch or m) // tm
    in_specs = [pl.BlockSpec((tm, d), lambda i: (i, 0)), _const_spec((1, d)), _const_spec((d, n))]
    args = [x, g.reshape(1, d).astype(F32), w]
    if aux is not None:
        in_specs.append(_const_spec(aux.shape))
        args.append(aux)
    out_shape, out_specs = [], []
    for (c0, c1, dtypes, _) in segs:
        for dt in dtypes:
            if dt == "T":
                out_shape.append(jax.ShapeDtypeStruct((m // rows_per_batch, c1 - c0, rows_per_batch), F32))
                out_specs.append(pl.BlockSpec((1, c1 - c0, tm), lambda i: (i // tiles_per_b, 0, i % tiles_per_b)))
            else:
                out_shape.append(jax.ShapeDtypeStruct((m, c1 - c0), dt))
                out_specs.append(pl.BlockSpec((tm, c1 - c0), lambda i: (i, 0)))
    return pl.pallas_call(
        functools.partial(_norm_proj_kernel, segs=tuple(segs), has_aux=aux is not None),
        grid=(m // tm,),
        in_specs=in_specs,
        out_specs=out_specs,
        out_shape=out_shape,
        compiler_params=_cparams(("parallel",)),
        name="norm_proj",
    )(*args)


def _mm_res_kernel(*refs, n, proj):
    a_refs, w_refs = refs[:n], refs[n:2 * n]
    res_ref = refs[2 * n]
    acc = res_ref[...]
    for a_ref, w_ref in zip(a_refs, w_refs):
        acc = acc + jnp.dot(a_ref[...].astype(BF16), w_ref[...], preferred_element_type=F32)
    if proj:
        g_ref, wp_ref, o_ref, p_ref = refs[2 * n + 1:]
        ms = jnp.mean(acc * acc, axis=-1, keepdims=True)
        xb = ((acc * lax.rsqrt(ms + RMS_EPS)) * g_ref[...]).astype(BF16)
        p_ref[...] = jnp.dot(xb, wp_ref[...], preferred_element_type=F32).astype(p_ref.dtype)
    else:
        o_ref = refs[2 * n + 1]
    o_ref[...] = acc


def matmul_residual(a_list, w_list, res, proj=None, tm=512):
    m, d = res.shape
    tm = min(tm, m)
    assert m % tm == 0
    n = len(a_list)
    in_specs = [pl.BlockSpec((tm, a.shape[1]), lambda i: (i, 0)) for a in a_list]
    in_specs += [_const_spec(w.shape) for w in w_list]
    in_specs.append(pl.BlockSpec((tm, d), lambda i: (i, 0)))
    args = [*a_list, *w_list, res]
    out_specs = pl.BlockSpec((tm, d), lambda i: (i, 0))
    out_shape = jax.ShapeDtypeStruct((m, d), F32)
    if proj is not None:
        g, wp = proj
        in_specs += [_const_spec((1, d)), _const_spec(wp.shape)]
        args += [g.reshape(1, d).astype(F32), wp]
        out_specs = [out_specs, pl.BlockSpec((tm, wp.shape[1]), lambda i: (i, 0))]
        out_shape = [out_shape, jax.ShapeDtypeStruct((m, wp.shape[1]), BF16)]
    return pl.pallas_call(
        functools.partial(_mm_res_kernel, n=n, proj=proj is not None),
        grid=(m // tm,),
        in_specs=in_specs,
        out_specs=out_specs,
        out_shape=out_shape,
        compiler_params=_cparams(("parallel",)),
        name="matmul_residual",
    )(*args)


FFN_CHUNK = 512


def _ffn_kernel(*refs, final_norm, pre):
    refs = list(refs)
    x_ref = refs.pop(0)
    x = x_ref[...]
    if pre:
        a_ref, wa_ref = refs.pop(0), refs.pop(0)
        x = x + jnp.dot(a_ref[...], wa_ref[...], preferred_element_type=F32)
    if final_norm:
        g_ref, wi_ref, wo_ref, gf_ref, o_ref, acc_ref = refs
    else:
        g_ref, wi_ref, wo_ref, o_ref, acc_ref = refs
    ms = jnp.mean(x * x, axis=-1, keepdims=True)
    xb = ((x * lax.rsqrt(ms + RMS_EPS)) * g_ref[...]).astype(BF16)
    hdim = wo_ref.shape[0]
    acc_ref[...] = x
    for c0 in range(0, hdim, FFN_CHUNK):
        c1 = min(c0 + FFN_CHUNK, hdim)
        gt = jnp.dot(xb, wi_ref[:, c0:c1], preferred_element_type=F32)
        up = jnp.dot(xb, wi_ref[:, hdim + c0:hdim + c1], preferred_element_type=F32)
        act = (_silu(gt) * up).astype(BF16)
        acc_ref[...] += jnp.dot(act, wo_ref[c0:c1, :], preferred_element_type=F32)
    y = acc_ref[...]
    if final_norm:
        ms2 = jnp.mean(y * y, axis=-1, keepdims=True)
        y = (y * lax.rsqrt(ms2 + RMS_EPS)) * gf_ref[...]
    o_ref[...] = y


def ffn(x, g, w_in, w_out, layer, g_final=None, pre=None, tm=512):
    m, d = x.shape
    tm = min(tm, m)
    assert m % tm == 0
    final_norm = g_final is not None
    in_specs = [pl.BlockSpec((tm, d), lambda i: (i, 0))]
    args = [x]
    if pre is not None:
        a, wa = pre
        in_specs += [pl.BlockSpec((tm, a.shape[1]), lambda i: (i, 0)), _const_spec(wa.shape)]
        args += [a, wa]
    layer_spec = lambda w: pl.BlockSpec((None,) + w.shape[1:], lambda i: (layer, 0, 0), pipeline_mode=pl.Buffered(1))
    in_specs += [_const_spec((1, d)), layer_spec(w_in), layer_spec(w_out)]
    args += [g.reshape(1, d).astype(F32), w_in, w_out]
    if final_norm:
        in_specs.append(_const_spec((1, d)))
        args.append(g_final.reshape(1, d).astype(F32))
    return pl.pallas_call(
        functools.partial(_ffn_kernel, final_norm=final_norm, pre=pre is not None),
        grid=(m // tm,),
        in_specs=in_specs,
        out_specs=pl.BlockSpec((tm, d), lambda i: (i, 0)),
        out_shape=jax.ShapeDtypeStruct((m, d), F32),
        scratch_shapes=[pltpu.VMEM((tm, d), F32)],
        compiler_params=_cparams(("parallel",)),
        name="ffn",
    )(*args)


def _ssd_kernel(*refs, L, valid, has_h0, nb):
    for bi in range(nb):
        _ssd_row(refs, bi, L=L, valid=valid, has_h0=has_h0)


def _ssd_row(refs, bi, *, L, valid, has_h0):
    if has_h0:
        (xbc_ref, z_ref, dtr_ref, tail_ref, h0_ref, cw_ref, cb_ref, dtb_ref, al_ref,
         dsk_ref, gn_ref, y_ref, hout_ref, xpad_all, hT_all) = refs
    else:
        (xbc_ref, z_ref, dtr_ref, tail_ref, cw_ref, cb_ref, dtb_ref, al_ref,
         dsk_ref, gn_ref, y_ref, hout_ref, xpad_all, hT_all) = refs
    xpad_scr = xpad_all.at[bi]
    hT_scr = hT_all.at[bi]
    c = pl.program_id(1)
    nc = pl.num_programs(1)

    @pl.when(c == 0)
    def _():
        xpad_scr[0:8, :] = tail_ref[bi]
        if has_h0:
            hT_scr[...] = h0_ref[bi].T
        else:
            hT_scr[...] = jnp.zeros(hT_scr.shape, F32)

    x = xbc_ref[bi]
    prev = xpad_scr[0:8, :]
    first = _iota((L, CONV_DIM), 0) == 0
    part = None
    for j in range(CONV_K - 1):
        carry = sum(prev[7 - i:8 - i, :] * cw_ref[j - i:j - i + 1, :] for i in range(j + 1))
        cur = x * cw_ref[j:j + 1, :]
        if part is not None:
            cur = cur + part
        part = jnp.where(first, carry, pltpu.roll(cur, 1, 0))
    acc = (cb_ref[...] + x * cw_ref[CONV_K - 1:CONV_K, :]) + part
    xpad_scr[0:8, :] = x[L - 8:L, :]
    xc = _silu(acc)
    xs = xc[:, :SSM_WIDTH]
    gn_w = SSM_GROUPS * SSM_N
    bm = xc[:, SSM_WIDTH:SSM_WIDTH + gn_w]
    cm = xc[:, SSM_WIDTH + gn_w:]

    dt = _softplus(dtr_ref[bi] + dtb_ref[...])
    if valid < L:
        dt = jnp.where(_iota((L, LANES), 0) < valid, dt, 0.0)
    a128 = -jnp.exp(al_ref[...])

    expand = (_iota((LANES, SSM_WIDTH), 1) // SSM_P == _iota((LANES, SSM_WIDTH), 0)).astype(BF16)
    tril = (_iota((L, L), 0) >= _iota((L, L), 1))
    tril_b = tril.astype(BF16)

    dt_exp = _dot01_right(dt, expand)
    cs = _dot01_left(tril_b, dt * a128)
    cs_exp = _dot01_right(cs, expand)
    cs_t = cs.T

    last = cs_exp[L - 1:L, :]
    ecs = jnp.exp(cs_exp)
    wend = jnp.exp(last - cs_exp)
    dec = jnp.exp(last)
    xdt = xs * dt_exp
    xdt_b = xdt.astype(BF16)
    xw_b = (xdt * wend).astype(BF16)

    lane_lo = _iota((L, LANES), 1) < SSM_P
    gw = SSM_WIDTH // SSM_GROUPS
    y_parts = []
    for g in range(SSM_GROUPS):
        bg = bm[:, g * SSM_N:(g + 1) * SSM_N]
        cg_b = cm[:, g * SSM_N:(g + 1) * SSM_N].astype(BF16)
        cb = lax.dot_general(cg_b, bg.astype(BF16), NT_DIMS, preferred_element_type=F32)
        h_old = hT_scr[:, g * gw:(g + 1) * gw]
        y_state = jnp.dot(cg_b, h_old.astype(BF16), preferred_element_type=F32)
        pairs = []
        for i in range(gw // LANES):
            slab = g * (gw // LANES) + i
            xp = xdt_b[:, slab * LANES:(slab + 1) * LANES]
            halves = []
            for half in range(2):
                h = 2 * slab + half
                diff = cs[:, h:h + 1] - cs_t[h:h + 1, :]
                dm = jnp.exp(jnp.where(tril, diff, NEG_INF))
                sc = (cb * dm).astype(BF16)
                halves.append(jnp.dot(sc, xp, preferred_element_type=F32))
            pairs.append(jnp.where(lane_lo, halves[0], halves[1]))
        y_intra = jnp.concatenate(pairs, axis=1)
        y_parts.append(y_intra + y_state * ecs[:, g * gw:(g + 1) * gw])
        bg_t = bg.T.astype(BF16)
        upd = jnp.dot(bg_t, xw_b[:, g * gw:(g + 1) * gw], preferred_element_type=F32)
        hT_scr[:, g * gw:(g + 1) * gw] = h_old * dec[:, g * gw:(g + 1) * gw] + upd
    y = jnp.concatenate(y_parts, axis=1) + dsk_ref[...] * xs
    yg = y * _silu(z_ref[bi])
    ms = jnp.mean(yg * yg, axis=-1, keepdims=True)
    y_ref[bi] = ((yg * lax.rsqrt(ms + RMS_EPS)) * gn_ref[...]).astype(y_ref.dtype)

    @pl.when(c == nc - 1)
    def _():
        hout_ref[bi] = hT_scr[...].T


def ssd(xbc, z, dtr, tail, h0, conv_w, conv_b, dt_bias, a_log, d_skip, ssm_norm, L, valid, nb=1):
    b, t, _ = xbc.shape
    assert t % L == 0 and b % nb == 0
    nc = t // L
    has_h0 = h0 is not None
    pad16 = lambda v: jnp.pad(v.astype(F32), (0, LANES - v.shape[0])).reshape(1, LANES)
    rep = lambda v: jnp.repeat(v.astype(F32), SSM_P).reshape(1, SSM_WIDTH)
    row = lambda last: pl.BlockSpec((nb, L, last), lambda i, j: (i, j, 0))
    per_b = lambda s1, s2: pl.BlockSpec((nb, s1, s2), lambda i, j: (i, 0, 0))
    in_specs = [row(CONV_DIM), row(SSM_WIDTH), row(LANES), per_b(8, CONV_DIM)]
    args = [xbc, z, dtr, tail]
    if has_h0:
        in_specs.append(per_b(SSM_WIDTH, SSM_N))
        args.append(h0)
    consts = [conv_w.astype(F32), conv_b.reshape(1, CONV_DIM).astype(F32), pad16(dt_bias), pad16(a_log),
              rep(d_skip), ssm_norm.reshape(1, SSM_WIDTH).astype(F32)]
    in_specs += [_const_spec(cst.shape) for cst in consts]
    args += consts
    return pl.pallas_call(
        functools.partial(_ssd_kernel, L=L, valid=valid, has_h0=has_h0, nb=nb),
        grid=(b // nb, nc),
        in_specs=in_specs,
        out_specs=[row(SSM_WIDTH), per_b(SSM_WIDTH, SSM_N)],
        out_shape=[jax.ShapeDtypeStruct((b, t, SSM_WIDTH), BF16),
                   jax.ShapeDtypeStruct((b, SSM_WIDTH, SSM_N), F32)],
        scratch_shapes=[pltpu.VMEM((nb, 8, CONV_DIM), F32), pltpu.VMEM((nb, SSM_N, SSM_WIDTH), F32)],
        compiler_params=_cparams(("parallel", "arbitrary")),
        name="ssd",
    )(*args)


GROUP = N_HEADS // N_KV


def _head_perm():
    order = []
    for slab in range(N_HEADS // 2):
        j, g = slab // GROUP, slab % GROUP
        order += [(2 * j) * GROUP + g, (2 * j + 1) * GROUP + g]
    return np.asarray(order)


HEAD_PERM = _head_perm()
COL_PERM = (HEAD_PERM[:, None] * HEAD_DIM + np.arange(HEAD_DIM)[None, :]).reshape(-1)
SLABS_PER_STEP = 4
HEADS_PER_STEP = 2 * SLABS_PER_STEP


LOG2E = math.log2(math.e)
Q_SCALE = LOG2E * HEAD_DIM ** -0.5


def _masked_q(q_ref, qm_scr, first_head):
    tq = q_ref.shape[1]
    lane = _iota((tq, LANES), 1)
    lo = lane < HEAD_DIM
    for s in range(SLABS_PER_STEP):
        slab = q_ref[0, :, s * LANES:(s + 1) * LANES]
        zero = jnp.zeros_like(slab)
        qm_scr[2 * s, :, 0:LANES] = jnp.where(lo, slab, zero)
        qm_scr[2 * s + 1, :, 0:LANES] = jnp.where(lo, zero, slab)
    for h in range(HEADS_PER_STEP):
        mine = (lane < BIAS_PARTS * N_HEADS) & (lane % N_HEADS == first_head + h)
        qm_scr[h, :, LANES:2 * LANES] = jnp.where(mine, -1.0, 0.0).astype(qm_scr.dtype)


def _write_heads(o_ref, acc, inv_l):
    tq = o_ref.shape[1]
    lo = _iota((tq, LANES), 1) < HEAD_DIM
    for s in range(SLABS_PER_STEP):
        a = acc[2 * s] * inv_l[2 * s]
        b = acc[2 * s + 1] * inv_l[2 * s + 1]
        o_ref[0, :, s * LANES:(s + 1) * LANES] = jnp.where(lo, a, b).astype(o_ref.dtype)


def _swa_prompt_kernel(q_ref, kp_ref, kc_ref, vp_ref, vc_ref, sink_ref, o_ref):
    i = pl.program_id(1)
    w = WINDOW
    r = _iota((w, 2 * w), 0)
    col = _iota((w, 2 * w), 1)
    lim = jnp.where(i > 0, r, 2 * w)
    mask = ((col > lim) & (col < w)) | ((col >= w) & (col - w <= r))
    lo = _iota((w, LANES), 1) < HEAD_DIM
    slab_of = lambda ref, s: ref[0, :, s * LANES:(s + 1) * LANES]
    npair = N_KV // 2
    kcat = [jnp.concatenate([slab_of(kp_ref, j), slab_of(kc_ref, j)], axis=0) for j in range(npair)]
    vcat = [jnp.concatenate([slab_of(vp_ref, j), slab_of(vc_ref, j)], axis=0) for j in range(npair)]
    heads = range(N_HEADS)
    scores = []
    for h in heads:
        qs = slab_of(q_ref, h // 2)
        zero = jnp.zeros_like(qs)
        qm = jnp.where(lo, qs, zero) if h % 2 == 0 else jnp.where(lo, zero, qs)
        s = lax.dot_general(qm, kcat[h // HEADS_PER_STEP], NT_DIMS, preferred_element_type=F32)
        scores.append(jnp.where(mask, s, NEG_INF))
    probs, inv_l = [], []
    for h in heads:
        sink = sink_ref[h:h + 1, :] * LOG2E
        m = jnp.maximum(jnp.max(scores[h], axis=-1, keepdims=True), sink)
        p = jnp.exp2(scores[h] - jnp.tile(m, (1, 2 * w // LANES)))
        inv_l.append(1.0 / (jnp.sum(p, axis=-1, keepdims=True) + jnp.exp2(sink - m)))
        probs.append(p.astype(BF16))
    acc = [jnp.dot(probs[h], vcat[h // HEADS_PER_STEP], preferred_element_type=F32) * inv_l[h] for h in heads]
    for s in range(N_HEADS // 2):
        o_ref[0, :, s * LANES:(s + 1) * LANES] = jnp.where(lo, acc[2 * s], acc[2 * s + 1]).astype(o_ref.dtype)


def swa_prompt(q, kb, vb, sink_perm):
    b, t, _ = q.shape
    w = WINDOW
    nb = t // w
    kvw = N_KV * HEAD_DIM
    sink_arr = jnp.broadcast_to(sink_perm.astype(F32).reshape(N_HEADS, 1), (N_HEADS, LANES))
    qspec = pl.BlockSpec((1, w, N_HEADS * HEAD_DIM), lambda bi, i: (bi, i, 0))
    cur = pl.BlockSpec((1, w, kvw), lambda bi, i: (bi, i, 0))
    prev = pl.BlockSpec((1, w, kvw), lambda bi, i: (bi, jnp.maximum(i - 1, 0), 0))
    return pl.pallas_call(
        _swa_prompt_kernel,
        grid=(b, nb),
        in_specs=[qspec, prev, cur, prev, cur, _const_spec((N_HEADS, LANES))],
        out_specs=qspec,
        out_shape=jax.ShapeDtypeStruct(q.shape, BF16),
        compiler_params=_cparams(("parallel", "arbitrary")),
        name="swa_prompt",
    )(q, kb, kb, vb, vb, sink_arr)


BIAS_PARTS = 3


def _logf_scan_kernel(lf_ref, parts_ref, lft_ref, carry_scr):
    c = pl.program_id(1)

    @pl.when(c == 0)
    def _():
        carry_scr[...] = jnp.zeros(carry_scr.shape, F32)

    L = lf_ref.shape[1]
    lf = lf_ref[0]
    tril_b = (_iota((L, L), 0) >= _iota((L, L), 1)).astype(BF16)
    cs = _dot01_left(tril_b, lf) + carry_scr[...]
    carry_scr[...] = cs[L - 1:L, :]
    src = _iota((LANES, LANES), 0)
    dst = _iota((LANES, LANES), 1)
    out = jnp.zeros((L, LANES), F32)
    for k, piece in enumerate(_split3(cs * LOG2E)):
        sel = ((src < N_HEADS) & (dst == src + k * N_HEADS)).astype(BF16)
        out = out + jnp.dot(piece, sel, preferred_element_type=F32)
    parts_ref[0] = out.astype(BF16)
    lft_ref[0] = lf.T[N_HEADS:2 * N_HEADS, :]


def logf_scan(lf):
    b, t, _ = lf.shape
    L = min(512, t)
    return pl.pallas_call(
        _logf_scan_kernel,
        grid=(b, t // L),
        in_specs=[pl.BlockSpec((1, L, LANES), lambda i, j: (i, j, 0))],
        out_specs=[pl.BlockSpec((1, L, LANES), lambda i, j: (i, j, 0)),
                   pl.BlockSpec((1, N_HEADS, L), lambda i, j: (i, 0, j))],
        out_shape=[jax.ShapeDtypeStruct((b, t, LANES), BF16), jax.ShapeDtypeStruct((b, N_HEADS, t), F32)],
        scratch_shapes=[pltpu.VMEM((1, LANES), F32)],
        compiler_params=_cparams(("parallel", "arbitrary")),
        name="logf_scan",
    )(lf)


def _fox_prompt_kernel(qi_ref, ki_ref, q_ref, k_ref, v_ref, c_ref, o_ref, qm_scr, m_scr, acc_scr, *, tq, tk):
    step = pl.program_id(2)
    qi = qi_ref[step]
    ki = ki_ref[step]
    last_k = (qi * tq + tq - 1) // tk
    heads = range(HEADS_PER_STEP)

    @pl.when(ki == 0)
    def _():
        _masked_q(q_ref, qm_scr, pl.program_id(1) * HEADS_PER_STEP)
        m_scr[...] = jnp.full(m_scr.shape, NEG_INF, F32)
        acc_scr[...] = jnp.zeros(acc_scr.shape, F32)

    def body(r0, r1, nk, masked):
        rows = slice(r0, r1)
        kb = jnp.concatenate([k_ref[0, 0:nk, :], c_ref[0, 0:nk, :]], axis=1)
        vb = v_ref[0, 0:nk, :]
        vb_ext = jnp.concatenate([vb, jnp.ones_like(vb)], axis=1)
        if masked:
            row = qi * tq + r0 + _iota((r1 - r0, nk), 0)
            col = ki * tk + _iota((r1 - r0, nk), 1)
            keep = col <= row
        scores = []
        for h in heads:
            s = lax.dot_general(qm_scr[h, rows, :], kb, NT_DIMS, preferred_element_type=F32)
            scores.append(jnp.where(keep, s, NEG_INF) if masked else s)
        probs, alphas = [], []
        for h in heads:
            m_prev = m_scr[h, rows, :]
            m_new = jnp.maximum(m_prev, jnp.max(scores[h], axis=-1, keepdims=True))
            alphas.append(jnp.exp2(m_prev - m_new))
            probs.append(jnp.exp2(scores[h] - jnp.tile(m_new, (1, nk // LANES))).astype(BF16))
            m_scr[h, rows, :] = m_new
        for h in heads:
            acc_scr[h, rows, :] = (jnp.tile(alphas[h], (1, 2)) * acc_scr[h, rows, :]
                                   + jnp.dot(probs[h], vb_ext, preferred_element_type=F32))

    on_diagonal = (ki + 1) * tk - 1 > qi * tq

    @pl.when(on_diagonal)
    def _():
        if tq == tk and tq % (2 * LANES) == 0:
            body(0, tq // 2, tk // 2, True)
            body(tq // 2, tq, tk, True)
        else:
            body(0, tq, tk, True)

    @pl.when(jnp.logical_not(on_diagonal))
    def _():
        body(0, tq, tk, False)

    @pl.when(ki == last_k)
    def _():
        _write_heads(o_ref, [acc_scr[h, :, 0:LANES] for h in heads],
                     [1.0 / acc_scr[h, :, LANES:2 * LANES] for h in heads])


def fox_prompt(q, kb, vb, bias_parts, tq=512, tk=512):
    b, t, _ = q.shape
    tq, tk = min(tq, t), min(tk, t)
    npair = N_KV // 2
    qi_l, ki_l = [], []
    for qi in range(t // tq):
        for ki in range((qi * tq + tq - 1) // tk + 1):
            qi_l.append(qi)
            ki_l.append(ki)
    qi_arr = jnp.asarray(qi_l, jnp.int32)
    ki_arr = jnp.asarray(ki_l, jnp.int32)
    qspec = pl.BlockSpec((1, tq, SLABS_PER_STEP * LANES), lambda bi, j, s, qi, ki: (bi, qi[s], j))
    kspec = pl.BlockSpec((1, tk, LANES), lambda bi, j, s, qi, ki: (bi, ki[s], j))
    cspec = pl.BlockSpec((1, tk, LANES), lambda bi, j, s, qi, ki: (bi, ki[s], 0))
    grid_spec = pltpu.PrefetchScalarGridSpec(
        num_scalar_prefetch=2,
        grid=(b, npair, len(qi_l)),
        in_specs=[qspec, kspec, kspec, cspec],
        out_specs=qspec,
        scratch_shapes=[pltpu.VMEM((HEADS_PER_STEP, tq, 2 * LANES), BF16),
                        pltpu.VMEM((HEADS_PER_STEP, tq, LANES), F32),
                        pltpu.VMEM((HEADS_PER_STEP, tq, 2 * LANES), F32)],
    )
    return pl.pallas_call(
        functools.partial(_fox_prompt_kernel, tq=tq, tk=tk),
        grid_spec=grid_spec,
        out_shape=jax.ShapeDtypeStruct(q.shape, BF16),
        compiler_params=_cparams(("parallel", "parallel", "arbitrary")),
        name="fox_prompt",
    )(qi_arr, ki_arr, q, kb, vb, bias_parts)


def _xattn_kernel(q_ref, k_ref, v_ref, o_ref, *, nb, mem, interleaved):
    scale = LOG2E * XA_DIM ** -0.5
    pairs = [(bi, h) for bi in range(nb) for h in range(XA_HEADS)]
    lanes = lambda h: slice(h * XA_DIM, (h + 1) * XA_DIM)

    def kv_head(ref, bi, h):
        if interleaved:
            return ref[bi, pl.ds(h, mem, stride=XA_HEADS), :].astype(BF16)
        return ref[bi, :, lanes(h)].astype(BF16)

    scores = [lax.dot_general(q_ref[bi, :, lanes(h)], kv_head(k_ref, bi, h), NT_DIMS,
                              preferred_element_type=F32) * scale for bi, h in pairs]
    probs, inv_l = [], []
    for s in scores:
        p = jnp.exp2(s - jnp.max(s, axis=-1, keepdims=True))
        inv_l.append(1.0 / jnp.sum(p, axis=-1, keepdims=True))
        probs.append(p.astype(BF16))
    for (bi, h), p, il in zip(pairs, probs, inv_l):
        o = jnp.dot(p, kv_head(v_ref, bi, h), preferred_element_type=F32) * il
        o_ref[bi, :, lanes(h)] = o.astype(o_ref.dtype)


def xattn(q, mk, mv, tq, nb, kv_first=0):
    b, t, _ = q.shape
    interleaved = mk.shape[2] == XA_DIM
    mem = mk.shape[1] // XA_HEADS if interleaved else mk.shape[1]
    assert kv_first % nb == 0
    off = kv_first // nb
    qspec = pl.BlockSpec((nb, tq, XA_WIDTH), lambda i, j: (i, j, 0))
    mspec = pl.BlockSpec((nb,) + mk.shape[1:], lambda i, j: (i + off, 0, 0))
    return pl.pallas_call(
        functools.partial(_xattn_kernel, nb=nb, mem=mem, interleaved=interleaved),
        grid=(b // nb, t // tq),
        in_specs=[qspec, mspec, mspec],
        out_specs=qspec,
        out_shape=jax.ShapeDtypeStruct(q.shape, BF16),
        compiler_params=_cparams(("parallel", "arbitrary")),
        name="xattn",
    )(q, mk, mv)


def _block_diag_q(q, nb, t):
    q4 = q.reshape(nb, t * N_HEADS, HEAD_DIM)
    qt = jnp.tile(q4, (1, 1, N_KV))
    row_kv = (np.arange(t * N_HEADS) % N_HEADS) // GROUP
    keep = jnp.asarray(row_kv[:, None] == (np.arange(N_KV * HEAD_DIM) // HEAD_DIM)[None, :])
    return jnp.where(keep[None], qt * jnp.asarray(HEAD_DIM ** -0.5, BF16), jnp.zeros((), BF16))


def _diag_extract(acc, rows):
    row_kv = (_iota((rows, HEAD_DIM), 0) % N_HEADS) // GROUP
    out = jnp.zeros((rows, HEAD_DIM), F32)
    for kv in range(N_KV):
        out = out + jnp.where(row_kv == kv, acc[:, kv * HEAD_DIM:(kv + 1) * HEAD_DIM], 0.0)
    return out


NEW_PAD = 16


def _swa_sample_kernel(q_ref, ck_ref, cv_ref, kn_ref, vn_ref, sink_ref, o_ref, ok_ref, ov_ref, *, nb, t):
    w = WINDOW
    rows = t * N_HEADS
    tok_c = _iota((rows, w), 0) // N_HEADS
    keep_c = _iota((rows, w), 1) > tok_c
    keep_n = _iota((rows, NEW_PAD), 1) <= _iota((rows, NEW_PAD), 0) // N_HEADS
    sink = sink_ref[:, 0:1]
    place = (_iota((w, NEW_PAD), 0) == _iota((w, NEW_PAD), 1) + (w - t)).astype(BF16)
    old_slots = _iota((N_KV * HEAD_DIM, w), 1) < w - t
    for bi in range(nb):
        ck = ck_ref[bi]
        cv = cv_ref[bi]
        kn = kn_ref[bi]
        vn = vn_ref[bi]
        q = q_ref[bi]
        s_c = jnp.where(keep_c, jnp.dot(q, ck.astype(BF16), preferred_element_type=F32), NEG_INF)
        s_n = jnp.where(keep_n, lax.dot_general(q, kn.astype(BF16), NT_DIMS, preferred_element_type=F32), NEG_INF)
        m = jnp.maximum(jnp.maximum(jnp.max(s_c, axis=-1, keepdims=True), jnp.max(s_n, axis=-1, keepdims=True)), sink)
        p_c = jnp.exp(s_c - m)
        p_n = jnp.exp(s_n - m)
        l = jnp.sum(p_c, axis=-1, keepdims=True) + jnp.sum(p_n, axis=-1, keepdims=True) + jnp.exp(sink - m)
        acc = (lax.dot_general(p_c.astype(BF16), cv.astype(BF16), NT_DIMS, preferred_element_type=F32)
               + jnp.dot(p_n.astype(BF16), vn.astype(BF16), preferred_element_type=F32))
        o_ref[bi] = (_diag_extract(acc, rows) * (1.0 / l)).astype(o_ref.dtype)
        ok_ref[bi] = jnp.where(old_slots, pltpu.roll(ck, w - t, 1), _dot01_left(place, kn).T)
        ov_ref[bi] = jnp.where(old_slots, pltpu.roll(cv, w - t, 1), _dot01_left(place, vn).T)


def swa_sample(qbd, ck_t, cv_t, kn, vn, sink, t, nb=8):
    b = qbd.shape[0]
    w = WINDOW
    rows = t * N_HEADS
    sink_rows = jnp.broadcast_to(jnp.tile(sink.astype(F32), t).reshape(rows, 1), (rows, LANES))
    bspec = lambda s1, s2: pl.BlockSpec((nb, s1, s2), lambda i: (i, 0, 0))
    kvw = N_KV * HEAD_DIM
    return pl.pallas_call(
        functools.partial(_swa_sample_kernel, nb=nb, t=t),
        grid=(b // nb,),
        in_specs=[bspec(rows, kvw), bspec(kvw, w), bspec(kvw, w), bspec(NEW_PAD, kvw), bspec(NEW_PAD, kvw),
                  _const_spec((rows, LANES))],
        out_specs=[bspec(rows, HEAD_DIM), bspec(kvw, w), bspec(kvw, w)],
        out_shape=[jax.ShapeDtypeStruct((b, rows, HEAD_DIM), BF16),
                   jax.ShapeDtypeStruct((b, kvw, w), F32), jax.ShapeDtypeStruct((b, kvw, w), F32)],
        compiler_params=_cparams(("parallel",)),
        name="swa_sample",
    )(qbd, ck_t, cv_t, kn, vn, sink_rows)


FOX_PAGES_PER_STEP = 64


def _fox_sample_kernel(pt_ref, q_ref, kn_ref, vn_ref, lfn_ref, kc_hbm, vc_hbm, lf_hbm, o_ref,
                       kbuf, vbuf, lbuf, sem, carry_scr, m_scr, l_scr, acc_scr,
                       *, t, n_pages, pg):
    b = pl.program_id(0)
    c = pl.program_id(1)
    nch = pl.num_programs(1)
    step = b * nch + c
    total = pl.num_programs(0) * nch
    rows = t * N_HEADS
    nh = N_HEADS

    def copies(stp, slot):
        base = (stp // nch) * n_pages + (stp % nch) * pg
        out = []
        for p in range(pg):
            page = pt_ref[base + p]
            out.append(pltpu.make_async_copy(kc_hbm.at[page], kbuf.at[slot, :, pl.ds(p * PAGE, PAGE)], sem.at[slot, 0]))
            out.append(pltpu.make_async_copy(vc_hbm.at[page], vbuf.at[slot, :, pl.ds(p * PAGE, PAGE)], sem.at[slot, 1]))
            out.append(pltpu.make_async_copy(lf_hbm.at[page], lbuf.at[slot, pl.ds(p * nh, nh)], sem.at[slot, 2]))
        return out

    slot = step % 2

    @pl.when(step == 0)
    def _():
        for cp in copies(step, slot):
            cp.start()

    @pl.when(step + 1 < total)
    def _():
        for cp in copies(step + 1, 1 - slot):
            cp.start()

    @pl.when(c == 0)
    def _():
        carry_scr[...] = jnp.zeros(carry_scr.shape, F32)
        m_scr[...] = jnp.full(m_scr.shape, NEG_INF, F32)
        l_scr[...] = jnp.zeros(l_scr.shape, F32)
        acc_scr[...] = jnp.zeros(acc_scr.shape, F32)

    for cp in copies(step, slot):
        cp.wait()

    q = q_ref[0]
    upper_b = (_iota((PAGE, PAGE), 0) <= _iota((PAGE, PAGE), 1)).astype(BF16)

    def online(s, pv):
        m_prev = m_scr[...]
        m_new = jnp.maximum(m_prev, jnp.max(s, axis=-1, keepdims=True))
        alpha = jnp.exp(m_prev - m_new)
        p = jnp.exp(s - m_new)
        l_scr[...] = alpha * l_scr[...] + jnp.sum(p, axis=-1, keepdims=True)
        acc_scr[...] = alpha * acc_scr[...] + pv(p.astype(BF16))
        m_scr[...] = m_new

    lf = lbuf[slot]
    cl = _dot01_right(lf, upper_b)
    tot = jnp.broadcast_to(cl[:, PAGE - 1:PAGE], cl.shape)
    r = _iota((pg * nh, pg * nh), 0)
    cc = _iota((pg * nh, pg * nh), 1)
    earlier_b = ((r % nh == cc % nh) & (cc // nh < r // nh)).astype(BF16)
    cum = cl + _dot01_left(earlier_b, tot) + jnp.concatenate([carry_scr[...]] * pg, axis=0)
    carry_new = jnp.broadcast_to(cum[(pg - 1) * nh:pg * nh, PAGE - 1:PAGE], (nh, LANES))
    carry_scr[...] = carry_new
    bias16 = jnp.concatenate([cum[p * nh:(p + 1) * nh, :] for p in range(pg)], axis=1)
    bias = jnp.concatenate([bias16] * t, axis=0)

    kc = kbuf[slot].astype(BF16)
    vc = vbuf[slot].astype(BF16)
    s = jnp.dot(q, kc, preferred_element_type=F32) - bias
    online(s, lambda p: lax.dot_general(p, vc, NT_DIMS, preferred_element_type=F32))

    @pl.when(c == nch - 1)
    def _():
        zpad = jnp.zeros((PAGE - NEW_PAD, N_KV * HEAD_DIM), BF16)
        knew = jnp.concatenate([kn_ref[0].astype(BF16), zpad], axis=0)
        vnew = jnp.concatenate([vn_ref[0].astype(BF16), zpad], axis=0)
        cum_n = _dot01_right(lfn_ref[0], upper_b) + carry_new
        sn = lax.dot_general(q, knew, NT_DIMS, preferred_element_type=F32)
        sn = sn - jnp.concatenate([cum_n] * t, axis=0)
        tok = _iota((rows, PAGE), 0) // nh
        sn = jnp.where(_iota((rows, PAGE), 1) <= tok, sn, NEG_INF)
        online(sn, lambda p: jnp.dot(p, vnew, preferred_element_type=F32))
        o_ref[0] = (_diag_extract(acc_scr[...], rows) * (1.0 / l_scr[...])).astype(o_ref.dtype)


def fox_sample(page_table, qbd, kn, vn, lfn_t, kcache, vcache, lf_t_cache, t, pg=FOX_PAGES_PER_STEP):
    b, n_pages = page_table.shape
    pg = min(pg, n_pages)
    assert n_pages % pg == 0
    rows = t * N_HEADS
    kvw = N_KV * HEAD_DIM
    bspec = lambda s1, s2: pl.BlockSpec((1, s1, s2), lambda i, j, pt: (i, 0, 0))
    any_spec = pl.BlockSpec(memory_space=pl.ANY)
    grid_spec = pltpu.PrefetchScalarGridSpec(
        num_scalar_prefetch=1,
        grid=(b, n_pages // pg),
        in_specs=[bspec(rows, kvw), bspec(NEW_PAD, kvw), bspec(NEW_PAD, kvw), bspec(N_HEADS, LANES),
                  any_spec, any_spec, any_spec],
        out_specs=bspec(rows, HEAD_DIM),
        scratch_shapes=[pltpu.VMEM((2, kvw, pg * PAGE), F32), pltpu.VMEM((2, kvw, pg * PAGE), F32),
                        pltpu.VMEM((2, pg * N_HEADS, LANES), F32), pltpu.SemaphoreType.DMA((2, 3)),
                        pltpu.VMEM((N_HEADS, LANES), F32),
                        pltpu.VMEM((rows, 1), F32), pltpu.VMEM((rows, 1), F32), pltpu.VMEM((rows, kvw), F32)],
    )
    return pl.pallas_call(
        functools.partial(_fox_sample_kernel, t=t, n_pages=n_pages, pg=pg),
        grid_spec=grid_spec,
        out_shape=jax.ShapeDtypeStruct((b, rows, HEAD_DIM), BF16),
        compiler_params=_cparams(("arbitrary", "arbitrary")),
        name="fox_sample",
    )(page_table.reshape(-1), qbd, kn, vn, lfn_t, kcache, vcache, lf_t_cache)


def _pad_cols(w, width):
    return jnp.pad(w, ((0, 0), (0, width - w.shape[1])))


def kernel(x_prompt, x_sample, state_ssm, state_conv, cache_swa_k, cache_swa_v, cache_fox_k, cache_fox_v,
           cache_fox_logf, cache_mem_k, cache_mem_v, page_table, mem_prompt, norm_mix, norm_xa, norm_mem,
           norm_ffn, w_in_even, conv_w, conv_b, dt_bias, a_log, d_skip, ssm_norm, swa_sink, w_out_even,
           w_in_odd, fox_fb, w_out_odd, w_xq, w_xk, w_xv, w_xo, w_ffn_in, w_ffn_out, norm_final):
    bp, seq, d = x_prompt.shape
    bs, ts, _ = x_sample.shape
    depth = norm_mix.shape[0]
    mem_len = mem_prompt.shape[1]
    kvw = N_KV * HEAD_DIM
    qw = N_HEADS * HEAD_DIM
    xp = x_prompt.reshape(bp * seq, d)
    xs = x_sample.reshape(bs * ts, d)
    mem2 = mem_prompt.reshape(bp * mem_len, d)
    w_ffn_in_b = w_ffn_in.astype(BF16)
    w_ffn_out_b = w_ffn_out.astype(BF16)
    outs = {k: [] for k in ("p_ssm", "p_conv", "p_swk", "p_swv", "p_fk", "p_fv", "p_fl", "p_mk", "p_mv",
                            "s_ssm", "s_conv", "s_swk", "s_swv", "s_fk", "s_fv", "s_fl")}

    for l in range(depth):
        li = l // 2
        if l % 2 == 0:
            w = w_in_even[li]
            o_z, o_xbc, o_dt = 0, SSM_WIDTH, SSM_WIDTH + CONV_DIM
            o_q = o_dt + SSM_HEADS
            o_k, o_v = o_q + qw, o_q + qw + kvw
            wz, wxbc, wdt = w[:, o_z:o_xbc], w[:, o_xbc:o_dt], _pad_cols(w[:, o_dt:o_q], LANES)
            wq, wk, wv = w[:, o_q:o_k], w[:, o_k:o_v], w[:, o_v:o_v + kvw]
            c_xbc, c_q, c_k, c_v, c_dt = SSM_WIDTH, SSM_WIDTH + CONV_DIM, SSM_WIDTH + CONV_DIM + qw, \
                SSM_WIDTH + CONV_DIM + qw + kvw, SSM_WIDTH + CONV_DIM + qw + 2 * kvw
            n_tot = c_dt + LANES
            wo = w_out_even[li]
            wo_ssm = wo[:SSM_WIDTH].astype(BF16)
            wo_att = wo[SSM_WIDTH:]
            ssm_params = (conv_w[li], conv_b[li], dt_bias[li], a_log[li], d_skip[li], ssm_norm[li])

            w_p = jnp.concatenate([wz, wxbc, wq[:, COL_PERM], wk, wv, wdt], axis=1).astype(BF16)
            segs_p = [(0, c_xbc, (F32,), None), (c_xbc, c_q, (F32,), None), (c_q, c_k, (BF16,), Q_SCALE),
                      (c_k, c_v, (F32, BF16), None), (c_v, c_dt, (F32, BF16), None), (c_dt, n_tot, (F32,), None)]
            z, xbc, q, k, kb, v, vb, dtr = norm_proj(xp, norm_mix[l], w_p, segs_p)
            xbc3 = xbc.reshape(bp, seq, CONV_DIM)
            y_ssm, h = ssd(xbc3, z.reshape(bp, seq, SSM_WIDTH), dtr.reshape(bp, seq, LANES),
                           jnp.zeros((bp, 8, CONV_DIM), F32), None, *ssm_params, L=SSM_CHUNK, valid=SSM_CHUNK,
                           nb=math.gcd(bp, 4))
            y_att = swa_prompt(q.reshape(bp, seq, qw), kb.reshape(bp, seq, kvw), vb.reshape(bp, seq, kvw),
                               swa_sink[li][HEAD_PERM])
            mix_p = ([y_ssm.reshape(bp * seq, SSM_WIDTH), y_att.reshape(bp * seq, qw)],
                     [wo_ssm, wo_att[COL_PERM].astype(BF16)])
            last_w = lambda a: a.reshape(bp, seq, kvw)[:, seq - WINDOW:].reshape(bp, WINDOW, N_KV, HEAD_DIM)
            outs["p_ssm"].append(h.reshape(bp, SSM_HEADS, SSM_P, SSM_N))
            outs["p_conv"].append(xbc3[:, seq - (CONV_K - 1):])
            outs["p_swk"].append(last_w(k))
            outs["p_swv"].append(last_w(v))

            w_s = jnp.concatenate([wz, wxbc, wq, wk, wv, wdt], axis=1).astype(BF16)
            segs_s = [(0, c_xbc, (F32,), None), (c_xbc, c_q, (F32,), None), (c_q, c_k, (BF16,), None),
                      (c_k, c_v, (F32,), None), (c_v, c_dt, (F32,), None), (c_dt, n_tot, (F32,), None)]
            z, xbc, q, k, v, dtr = norm_proj(xs, norm_mix[l], w_s, segs_s)
            ls = NEW_PAD
            pad_t = lambda a: jnp.pad(a.reshape(bs, ts, a.shape[-1]), ((0, 0), (0, ls - ts), (0, 0)))
            xbc3 = xbc.reshape(bs, ts, CONV_DIM)
            tail = jnp.pad(state_conv[li].astype(F32), ((0, 0), (8 - (CONV_K - 1), 0), (0, 0)))
            y_ssm, h = ssd(pad_t(xbc), pad_t(z), pad_t(dtr), tail, state_ssm[li].reshape(bs, SSM_WIDTH, SSM_N),
                           *ssm_params, L=ls, valid=ts, nb=math.gcd(bs, 8))
            y_ssm = y_ssm[:, :ts].reshape(bs * ts, SSM_WIDTH)
            kn, vn = pad_t(k), pad_t(v)
            win_t = lambda c: jnp.transpose(c[li].astype(F32), (0, 2, 3, 1)).reshape(bs, kvw, WINDOW)
            o_att, new_k, new_v = swa_sample(_block_diag_q(q, bs, ts), win_t(cache_swa_k), win_t(cache_swa_v),
                                             kn, vn, swa_sink[li], ts)
            win_back = lambda c: jnp.transpose(c.reshape(bs, N_KV, HEAD_DIM, WINDOW), (0, 3, 1, 2))
            mix_s = ([y_ssm, o_att.reshape(bs * ts, qw)], [wo_ssm, wo_att.astype(BF16)])
            outs["s_ssm"].append(h.reshape(bs, SSM_HEADS, SSM_P, SSM_N))
            outs["s_conv"].append(jnp.concatenate([state_conv[li].astype(F32), xbc3], axis=1)[:, -(CONV_K - 1):])
            outs["s_swk"].append(win_back(new_k))
            outs["s_swv"].append(win_back(new_v))
        else:
            w = w_in_odd[li]
            wq, wk, wv, wf = w[:, :qw], w[:, qw:qw + kvw], w[:, qw + kvw:qw + 2 * kvw], w[:, qw + 2 * kvw:]
            c_k, c_v, c_f = qw, qw + kvw, qw + 2 * kvw
            n_tot = c_f + LANES
            fb = fox_fb[li].astype(F32)

            wf_p = _pad_cols(jnp.concatenate([wf[:, HEAD_PERM], wf], axis=1), LANES)
            fb_p = jnp.pad(jnp.concatenate([fb[HEAD_PERM], fb]), (0, LANES - 2 * N_HEADS)).reshape(1, LANES)
            w_p = jnp.concatenate([wq[:, COL_PERM], wk, wv, wf_p], axis=1).astype(BF16)
            segs_p = [(0, c_k, (BF16,), Q_SCALE), (c_k, c_v, ("T", BF16), None), (c_v, c_f, ("T", BF16), None),
                      (c_f, n_tot, (F32,), "logsig")]
            q, k_t, kb, v_t, vb, lf = norm_proj(xp, norm_mix[l], w_p, segs_p, aux=fb_p, rows_per_batch=seq)
            lf3 = lf.reshape(bp, seq, LANES)
            bias_parts, lf_t = logf_scan(lf3)
            y = fox_prompt(q.reshape(bp, seq, qw), kb.reshape(bp, seq, kvw), vb.reshape(bp, seq, kvw), bias_parts)
            mix_p = ([y.reshape(bp * seq, qw)], [w_out_odd[li][COL_PERM].astype(BF16)])
            untr = lambda a: jnp.transpose(a.reshape(bp, N_KV, HEAD_DIM, seq), (0, 3, 1, 2))
            outs["p_fk"].append(untr(k_t))
            outs["p_fv"].append(untr(v_t))
            outs["p_fl"].append(jnp.swapaxes(lf_t, 1, 2))

            w_s = jnp.concatenate([wq, wk, wv, _pad_cols(wf, LANES)], axis=1).astype(BF16)
            fb_s = jnp.pad(fb, (0, LANES - N_HEADS)).reshape(1, LANES)
            segs_s = [(0, c_k, (BF16,), None), (c_k, c_v, (F32,), None), (c_v, c_f, (F32,), None),
                      (c_f, n_tot, (F32,), "logsig")]
            q, k, v, lf = norm_proj(xs, norm_mix[l], w_s, segs_s, aux=fb_s)
            lf_new = lf[:, :N_HEADS].reshape(bs, ts, N_HEADS)
            pad_t = lambda a: jnp.pad(a.reshape(bs, ts, a.shape[-1]), ((0, 0), (0, NEW_PAD - ts), (0, 0)))
            lfn_t = jnp.pad(jnp.swapaxes(lf_new, 1, 2), ((0, 0), (0, 0), (0, LANES - ts)))
            pool = cache_fox_k.shape[1]
            page_t = lambda c: jnp.transpose(c[li].astype(F32), (0, 2, 3, 1)).reshape(pool, kvw, PAGE)
            y = fox_sample(page_table, _block_diag_q(q, bs, ts), pad_t(k), pad_t(v), lfn_t,
                           page_t(cache_fox_k), page_t(cache_fox_v),
                           jnp.swapaxes(cache_fox_logf[li].astype(F32), 1, 2), ts)
            mix_s = ([y.reshape(bs * ts, qw)], [w_out_odd[li].astype(BF16)])
            outs["s_fk"].append(k.reshape(bs, ts, N_KV, HEAD_DIM))
            outs["s_fv"].append(v.reshape(bs, ts, N_KV, HEAD_DIM))
            outs["s_fl"].append(lf_new)

        wkv = jnp.concatenate([w_xk[l], w_xv[l]], axis=1).astype(BF16)
        mk, mv = norm_proj(mem2, norm_mem[l], wkv, [(0, XA_WIDTH, (F32,), None), (XA_WIDTH, 2 * XA_WIDTH, (F32,), None)])
        wq_b = w_xq[l].astype(BF16)
        wo_b = w_xo[l].astype(BF16)
        xp, qx = matmul_residual(*mix_p, xp, proj=(norm_xa[l], wq_b))
        ox_p = xattn(qx.reshape(bp, seq, XA_WIDTH), mk.reshape(bp, mem_len, XA_WIDTH),
                     mv.reshape(bp, mem_len, XA_WIDTH), tq=512, nb=1).reshape(bp * seq, XA_WIDTH)
        xs, qx = matmul_residual(*mix_s, xs, proj=(norm_xa[l], wq_b))
        qx = jnp.pad(qx.reshape(bs, ts, XA_WIDTH), ((0, 0), (0, NEW_PAD - ts), (0, 0)))
        all_layers = lambda c: c.astype(F32).reshape(depth * bs, mem_len * XA_HEADS, XA_DIM)
        ox_s = xattn(qx, all_layers(cache_mem_k), all_layers(cache_mem_v), tq=NEW_PAD, nb=4, kv_first=l * bs)
        ox_s = ox_s[:, :ts].reshape(bs * ts, XA_WIDTH)
        outs["p_mk"].append(mk.reshape(bp, mem_len, XA_HEADS, XA_DIM))
        outs["p_mv"].append(mv.reshape(bp, mem_len, XA_HEADS, XA_DIM))

        gf = norm_final if l == depth - 1 else None
        xp = ffn(xp, norm_ffn[l], w_ffn_in_b, w_ffn_out_b, l, g_final=gf, pre=(ox_p, wo_b))
        xs = ffn(xs, norm_ffn[l], w_ffn_in_b, w_ffn_out_b, l, g_final=gf, pre=(ox_s, wo_b))

    st = lambda key: jnp.stack(outs[key])
    return (xp.reshape(bp, seq, d), xs.reshape(bs, ts, d),
            st("p_ssm"), st("p_conv"), st("p_swk"), st("p_swv"), st("p_fk"), st("p_fv"), st("p_fl"),
            st("p_mk"), st("p_mv"), st("s_ssm"), st("s_conv"), st("s_swk"), st("s_swv"),
            st("s_fk"), st("s_fv"), st("s_fl"))
```
